```python
import math
import jax, jax.numpy as jnp
from jax import lax
import numpy as np

D_MODEL = 1024
BATCH = 2
SEQ = 8192
DEPTH = 2
DEC_BATCH = 128
DEC_SEQ = 1
PAST_LEN = 16384
PAGE_SIZE = 128

HEAD_DIM = 64
N_HEADS_FOX = D_MODEL // HEAD_DIM
N_KV_FOX = N_HEADS_FOX // 2
N_HEADS_SWA = D_MODEL // HEAD_DIM
N_KV_SWA = N_HEADS_SWA // 8
WINDOW = 128
ROPE_THETA = 10000.0
D_FF = 2816
CONV_W = 3
Q_BLOCK = 128
KEY_PAGES_PER_BLOCK = 16
RMS_EPS = 1e-6
FORGET_BIAS_INIT = 3.0
N_MIXERS = 2
N_FOX_LAYERS = (DEPTH + 1) // 2
N_SWA_LAYERS = DEPTH // 2

kernel_name = 'fox_swa_sink_convffn_hybrid_step'


def rmsnorm(x, g):
    xf = x.astype(jnp.float32)
    y = xf * lax.rsqrt(jnp.mean(xf * xf, axis=-1, keepdims=True) + RMS_EPS)
    return (y * g.astype(jnp.float32)).astype(x.dtype)


def rope(x, pos):
    half = x.shape[-1] // 2
    inv = ROPE_THETA ** (-jnp.arange(half, dtype=jnp.float32) / half)
    ang = pos.astype(jnp.float32)[:, None] * inv[None, :]
    cos = jnp.cos(ang)[None, :, None, :]
    sin = jnp.sin(ang)[None, :, None, :]
    xf = x.astype(jnp.float32)
    x1, x2 = xf[..., :half], xf[..., half:]
    return jnp.concatenate([x1 * cos - x2 * sin, x2 * cos + x1 * sin], axis=-1).astype(x.dtype)


def sink_probs(logits, mask, sink):
    logits = jnp.where(mask, logits, -jnp.inf)
    m = jnp.maximum(jnp.max(logits, axis=-1, keepdims=True), sink)
    p = jnp.exp(logits - m)
    return p / (jnp.sum(p, axis=-1, keepdims=True) + jnp.exp(sink - m))


def fox_project(h, w_qkv, b_f):
    b, s, _ = h.shape
    g = N_HEADS_FOX // N_KV_FOX
    nq = N_HEADS_FOX * HEAD_DIM
    nk = N_KV_FOX * HEAD_DIM
    proj = jnp.einsum('bsd,de->bse', h, w_qkv)
    q = proj[..., :nq].reshape(b, s, N_KV_FOX, g, HEAD_DIM)
    k = proj[..., nq:nq + nk].reshape(b, s, N_KV_FOX, HEAD_DIM)
    v = proj[..., nq + nk:nq + 2 * nk].reshape(b, s, N_KV_FOX, HEAD_DIM)
    logf = jax.nn.log_sigmoid((proj[..., nq + 2 * nk:] + b_f).astype(jnp.float32))
    return q, k, v, logf


def fox_prompt(h, w_qkv, b_f, w_o):
    b, s, _ = h.shape
    g = N_HEADS_FOX // N_KV_FOX
    scale = HEAD_DIM ** -0.5
    q, k, v, logf = fox_project(h, w_qkv, b_f)
    c = jnp.cumsum(logf, axis=1).reshape(b, s, N_KV_FOX, g)
    c_key = jnp.transpose(c, (0, 2, 3, 1))
    key_pos = jnp.arange(s)

    def block(i):
        start = i * Q_BLOCK
        qb = lax.dynamic_slice_in_dim(q, start, Q_BLOCK, axis=1)
        cq = lax.dynamic_slice_in_dim(c_key, start, Q_BLOCK, axis=3)
        logits = jnp.einsum('bqkgd,bskd->bkgqs', qb, k, preferred_element_type=jnp.float32) * scale
        logits = logits + cq[..., :, None] - c_key[..., None, :]
        q_pos = start + jnp.arange(Q_BLOCK)
        logits = jnp.where(key_pos[None, :] <= q_pos[:, None], logits, -jnp.inf)
        p = jax.nn.softmax(logits, axis=-1)
        return jnp.einsum('bkgqs,bskd->bqkgd', p.astype(v.dtype), v)

    o = lax.map(block, jnp.arange(s // Q_BLOCK))
    o = jnp.transpose(o, (1, 0, 2, 3, 4, 5)).reshape(b, s, N_HEADS_FOX * HEAD_DIM)
    return o @ w_o, k, v, logf


def fox_sample(h, k_pages, v_pages, logf_pages, layer, page_table, w_qkv, b_f, w_o):
    bd, sd, _ = h.shape
    g = N_HEADS_FOX // N_KV_FOX
    scale = HEAD_DIM ** -0.5
    n_pages = page_table.shape[1]
    page = k_pages.shape[2]
    past = n_pages * page
    pb = math.gcd(n_pages, KEY_PAGES_PER_BLOCK)
    blk = pb * page
    q, k, v, logf = fox_project(h, w_qkv, b_f)
    logf_past = logf_pages[layer, page_table].reshape(bd, past, N_HEADS_FOX).astype(jnp.float32)
    c_past = jnp.cumsum(logf_past, axis=1)
    c_new = c_past[:, -1:] + jnp.cumsum(logf, axis=1)
    cq = jnp.transpose(c_new.reshape(bd, sd, N_KV_FOX, g), (0, 2, 3, 1))

    def past_block(j):
        pt = lax.dynamic_slice_in_dim(page_table, j * pb, pb, axis=1)
        kb = k_pages[layer, pt].reshape(bd, blk, N_KV_FOX, HEAD_DIM)
        vb = v_pages[layer, pt].reshape(bd, blk, N_KV_FOX, HEAD_DIM)
        cs = lax.dynamic_slice_in_dim(c_past, j * blk, blk, axis=1).reshape(bd, blk, N_KV_FOX, g)
        cs = jnp.transpose(cs, (0, 2, 3, 1))
        logits = jnp.einsum('bqkgd,bskd->bkgqs', q, kb, preferred_element_type=jnp.float32) * scale
        logits = logits + cq[..., :, None] - cs[..., None, :]
        m = jnp.max(logits, axis=-1, keepdims=True)
        p = jnp.exp(logits - m)
        acc = jnp.einsum('bkgqs,bskd->bkgqd', p, vb.astype(jnp.float32))
        return m, jnp.sum(p, axis=-1, keepdims=True), acc

    m_p, l_p, acc_p = lax.map(past_block, jnp.arange(n_pages // pb))
    logits_n = jnp.einsum('bqkgd,bskd->bkgqs', q, k, preferred_element_type=jnp.float32) * scale
    logits_n = logits_n + cq[..., :, None] - cq[..., None, :]
    causal = jnp.arange(sd)[None, :] <= jnp.arange(sd)[:, None]
    logits_n = jnp.where(causal, logits_n, -jnp.inf)
    m_n = jnp.max(logits_n, axis=-1, keepdims=True)
    p_n = jnp.exp(logits_n - m_n)
    l_n = jnp.sum(p_n, axis=-1, keepdims=True)
    acc_n = jnp.einsum('bkgqs,bskd->bkgqd', p_n, v.astype(jnp.float32))
    m_all = jnp.maximum(jnp.max(m_p, axis=0), m_n)
    w_p = jnp.exp(m_p - m_all)
    w_n = jnp.exp(m_n - m_all)
    o = (jnp.sum(w_p * acc_p, axis=0) + w_n * acc_n) / (jnp.sum(w_p * l_p, axis=0) + w_n * l_n)
    o = jnp.transpose(o, (0, 3, 1, 2, 4)).reshape(bd, sd, N_HEADS_FOX * HEAD_DIM).astype(h.dtype)
    return o @ w_o, k, v, logf


def swa_project(h, w_qkv, pos):
    b, s, _ = h.shape
    g = N_HEADS_SWA // N_KV_SWA
    nq = N_HEADS_SWA * HEAD_DIM
    nk = N_KV_SWA * HEAD_DIM
    proj = jnp.einsum('bsd,de->bse', h, w_qkv)
    q = rope(proj[..., :nq].reshape(b, s, N_HEADS_SWA, HEAD_DIM), pos).reshape(b, s, N_KV_SWA, g, HEAD_DIM)
    k = rope(proj[..., nq:nq + nk].reshape(b, s, N_KV_SWA, HEAD_DIM), pos)
    v = proj[..., nq + nk:].reshape(b, s, N_KV_SWA, HEAD_DIM)
    return q, k, v


def swa_prompt(h, w_qkv, sinks, w_o):
    b, s, _ = h.shape
    g = N_HEADS_SWA // N_KV_SWA
    scale = HEAD_DIM ** -0.5
    nb = s // WINDOW
    q, k, v = swa_project(h, w_qkv, jnp.arange(s))
    qb = q.reshape(b, nb, WINDOW, N_KV_SWA, g, HEAD_DIM)
    kb = k.reshape(b, nb, WINDOW, N_KV_SWA, HEAD_DIM)
    vb = v.reshape(b, nb, WINDOW, N_KV_SWA, HEAD_DIM)
    pad = jnp.zeros_like(kb[:, :1])
    k2 = jnp.concatenate([jnp.concatenate([pad, kb[:, :-1]], axis=1), kb], axis=2)
    v2 = jnp.concatenate([jnp.concatenate([pad, vb[:, :-1]], axis=1), vb], axis=2)
    logits = jnp.einsum('bnqkgd,bnskd->bnkgqs', qb, k2, preferred_element_type=jnp.float32) * scale
    kj = jnp.arange(2 * WINDOW)[None, :]
    delta = jnp.arange(WINDOW)[:, None] + WINDOW - kj
    band = (delta >= 0) & (delta <= WINDOW)
    valid = (jnp.arange(nb) > 0)[:, None, None] | (kj >= WINDOW)[None]
    mask = (band[None] & valid)[None, :, None, None]
    sink = sinks.astype(jnp.float32).reshape(N_KV_SWA, g)[:, :, None, None]
    p = sink_probs(logits, mask, sink)
    o = jnp.einsum('bnkgqs,bnskd->bnqkgd', p.astype(v2.dtype), v2).reshape(b, s, N_HEADS_SWA * HEAD_DIM)
    keep = min(WINDOW, s)
    return o @ w_o, k[:, s - keep:], v[:, s - keep:]


def swa_sample(h, k_buf, v_buf, past_len, w_qkv, sinks, w_o):
    bd, sd, _ = h.shape
    g = N_HEADS_SWA // N_KV_SWA
    scale = HEAD_DIM ** -0.5
    wb = k_buf.shape[1]
    pos = past_len + jnp.arange(sd)
    q, k, v = swa_project(h, w_qkv, pos)
    keys = jnp.concatenate([k_buf.astype(k.dtype), k], axis=1)
    vals = jnp.concatenate([v_buf.astype(v.dtype), v], axis=1)
    key_pos = past_len - wb + jnp.arange(wb + sd)
    delta = pos[:, None] - key_pos[None, :]
    mask = (delta >= 0) & (delta <= WINDOW)
    logits = jnp.einsum('bqkgd,bskd->bkgqs', q, keys, preferred_element_type=jnp.float32) * scale
    sink = sinks.astype(jnp.float32).reshape(N_KV_SWA, g)[:, :, None, None]
    p = sink_probs(logits, mask, sink)
    o = jnp.einsum('bkgqs,bskd->bqkgd', p.astype(vals.dtype), vals).reshape(bd, sd, N_HEADS_SWA * HEAD_DIM)
    return o @ w_o, keys[:, -wb:], vals[:, -wb:]


def conv_ffn(h, prev, w_in, conv_w, conv_b, w_out):
    u = jnp.einsum('bsd,df->bsf', h, w_in)
    s = u.shape[1]
    up = jnp.concatenate([prev.astype(u.dtype), u], axis=1)
    c = conv_b + conv_w[0] * up[:, 0:s]
    for j in range(1, CONV_W):
        c = c + conv_w[j] * up[:, j:j + s]
    y = jnp.einsum('bsf,fd->bsd', jax.nn.silu(c[..., :D_FF]) * c[..., D_FF:], w_out)
    return y, up[:, -(CONV_W - 1):]


def setup_inputs(seed: int = 0) -> dict:
    key = jax.random.key(seed)
    ks = jax.random.split(key, 24)
    f32 = jnp.float32
    n_pages = PAST_LEN // PAGE_SIZE
    n_used = DEC_BATCH * n_pages
    n_phys = n_used + n_used // 4
    wb = min(WINDOW, PAST_LEN)
    qkv_fox = (N_HEADS_FOX + 2 * N_KV_FOX) * HEAD_DIM + N_HEADS_FOX
    qkv_swa = (N_HEADS_SWA + 2 * N_KV_SWA) * HEAD_DIM

    def nrm(k, shape, scale):
        return jax.random.normal(k, shape, f32) * scale

    page_table = jax.random.permutation(ks[0], n_phys)[:n_used].reshape(DEC_BATCH, n_pages).astype(jnp.int32)
    return {
        'x_prompt': nrm(ks[1], (BATCH, SEQ, D_MODEL), 1.0),
        'x_sample': nrm(ks[2], (DEC_BATCH, DEC_SEQ, D_MODEL), 1.0),
        'cache_fox_k': nrm(ks[3], (N_FOX_LAYERS, n_phys, PAGE_SIZE, N_KV_FOX, HEAD_DIM), 1.0),
        'cache_fox_v': nrm(ks[4], (N_FOX_LAYERS, n_phys, PAGE_SIZE, N_KV_FOX, HEAD_DIM), 1.0),
        'cache_fox_logf': jax.nn.log_sigmoid(FORGET_BIAS_INIT + nrm(ks[5], (N_FOX_LAYERS, n_phys, PAGE_SIZE, N_HEADS_FOX), 1.0)),
        'cache_swa_k': nrm(ks[6], (N_SWA_LAYERS, DEC_BATCH, wb, N_KV_SWA, HEAD_DIM), 1.0),
        'cache_swa_v': nrm(ks[7], (N_SWA_LAYERS, DEC_BATCH, wb, N_KV_SWA, HEAD_DIM), 1.0),
        'state_ffn_conv': nrm(ks[8], (DEPTH, DEC_BATCH, CONV_W - 1, 2 * D_FF), 1.0),
        'page_table': page_table,
        'norm_mixer': 1.0 + nrm(ks[9], (DEPTH, D_MODEL), 0.1),
        'norm_ffn': 1.0 + nrm(ks[10], (DEPTH, D_MODEL), 0.1),
        'norm_final': 1.0 + nrm(ks[11], (D_MODEL,), 0.1),
        'fox_w_qkv': nrm(ks[12], (N_FOX_LAYERS, D_MODEL, qkv_fox), D_MODEL ** -0.5),
        'fox_b_f': FORGET_BIAS_INIT + nrm(ks[13], (N_FOX_LAYERS, N_HEADS_FOX), 0.5),
        'fox_w_o': nrm(ks[14], (N_FOX_LAYERS, N_HEADS_FOX * HEAD_DIM, D_MODEL), (N_HEADS_FOX * HEAD_DIM) ** -0.5),
        'swa_w_qkv': nrm(ks[15], (N_SWA_LAYERS, D_MODEL, qkv_swa), D_MODEL ** -0.5),
        'swa_sinks': nrm(ks[16], (N_SWA_LAYERS, N_HEADS_SWA), 1.0),
        'swa_w_o': nrm(ks[17], (N_SWA_LAYERS, N_HEADS_SWA * HEAD_DIM, D_MODEL), (N_HEADS_SWA * HEAD_DIM) ** -0.5),
        'ffn_w_in': nrm(ks[18], (DEPTH, D_MODEL, 2 * D_FF), D_MODEL ** -0.5),
        'ffn_conv_w': nrm(ks[19], (DEPTH, CONV_W, 2 * D_FF), CONV_W ** -0.5),
        'ffn_conv_b': nrm(ks[20], (DEPTH, 2 * D_FF), 0.02),
        'ffn_w_out': nrm(ks[21], (DEPTH, D_FF, D_MODEL), D_FF ** -0.5),
    }


def reference(x_prompt, x_sample, cache_fox_k, cache_fox_v, cache_fox_logf, cache_swa_k, cache_swa_v,
              state_ffn_conv, page_table, norm_mixer, norm_ffn, norm_final, fox_w_qkv, fox_b_f, fox_w_o,
              swa_w_qkv, swa_sinks, swa_w_o, ffn_w_in, ffn_conv_w, ffn_conv_b, ffn_w_out):
    past_len = page_table.shape[1] * cache_fox_k.shape[2]
    xp, xs = x_prompt, x_sample
    fkp, fvp, flp, fks, fvs, fls = [], [], [], [], [], []
    skp, svp, sks, svs = [], [], [], []
    cp_list, cs_list = [], []
    for i in range(DEPTH):
        j = i // N_MIXERS
        hp = rmsnorm(xp, norm_mixer[i])
        hs = rmsnorm(xs, norm_mixer[i])
        if i % N_MIXERS == 0:
            yp, kp, vp, lp = fox_prompt(hp, fox_w_qkv[j], fox_b_f[j], fox_w_o[j])
            ys, ks_, vs_, ls_ = fox_sample(hs, cache_fox_k, cache_fox_v, cache_fox_logf, j, page_table,
                                           fox_w_qkv[j], fox_b_f[j], fox_w_o[j])
            fkp.append(kp); fvp.append(vp); flp.append(lp)
            fks.append(ks_); fvs.append(vs_); fls.append(ls_)
        else:
            yp, kp, vp = swa_prompt(hp, swa_w_qkv[j], swa_sinks[j], swa_w_o[j])
            ys, ks_, vs_ = swa_sample(hs, cache_swa_k[j], cache_swa_v[j], past_len,
                                      swa_w_qkv[j], swa_sinks[j], swa_w_o[j])
            skp.append(kp); svp.append(vp); sks.append(ks_); svs.append(vs_)
        xp = xp + yp
        xs = xs + ys
        hp = rmsnorm(xp, norm_ffn[i])
        hs = rmsnorm(xs, norm_ffn[i])
        zeros_prev = jnp.zeros((xp.shape[0], CONV_W - 1, 2 * D_FF), dtype=xp.dtype)
        yp, cp = conv_ffn(hp, zeros_prev, ffn_w_in[i], ffn_conv_w[i], ffn_conv_b[i], ffn_w_out[i])
        ys, cs = conv_ffn(hs, state_ffn_conv[i], ffn_w_in[i], ffn_conv_w[i], ffn_conv_b[i], ffn_w_out[i])
        cp_list.append(cp); cs_list.append(cs)
        xp = xp + yp
        xs = xs + ys
    y_prompt = rmsnorm(xp, norm_final)
    y_sample = rmsnorm(xs, norm_final)
    return (y_prompt, y_sample,
            jnp.stack(fkp), jnp.stack(fvp), jnp.stack(flp),
            jnp.stack(fks), jnp.stack(fvs), jnp.stack(fls),
            jnp.stack(skp), jnp.stack(svp), jnp.stack(sks), jnp.stack(svs),
            jnp.stack(cp_list), jnp.stack(cs_list))
```

```python
import functools
import math

import jax
import jax.numpy as jnp
from jax import lax
from jax.experimental import pallas as pl
from jax.experimental.pallas import tpu as pltpu

HEAD_DIM = 64
N_HEADS = 16
N_KV_FOX = 8
N_KV_SWA = 2
WINDOW = 128
ROPE_THETA = 10000.0
RMS_EPS = 1e-6
CONV_W = 3
SCALE = HEAD_DIM ** -0.5

F32 = jnp.float32
BF16 = jnp.bfloat16

V7X_VMEM_LIMIT_BYTES = 56 * 1024 * 1024
LANES = 128
NEG_INF = float("-inf")

_NT = (((1,), (1,)), ((), ()))


def _cparams(*sem):
    return pltpu.CompilerParams(dimension_semantics=sem, vmem_limit_bytes=V7X_VMEM_LIMIT_BYTES)


def _rms_bf16(x, g):
    y = x * lax.rsqrt(jnp.mean(x * x, axis=-1, keepdims=True) + RMS_EPS)
    return (y * g).astype(BF16)


def _split3(x):
    hi = x.astype(BF16)
    r1 = x - hi.astype(F32)
    mid = r1.astype(BF16)
    lo = (r1 - mid.astype(F32)).astype(BF16)
    return hi, mid, lo


def _log_sigmoid(x):
    return jnp.minimum(x, 0.0) - jnp.log1p(jnp.exp(-jnp.abs(x)))


def _upper_tri(n):
    r = lax.broadcasted_iota(jnp.int32, (n, n), 0)
    c = lax.broadcasted_iota(jnp.int32, (n, n), 1)
    return (r <= c).astype(BF16)


def _fox_proj_kernel(x_ref, g_ref, wq_ref, wkvT_ref, wfT_ref, bf_ref, tri_ref,
                     q_ref, kT_ref, vT_ref, kTb_ref, vTb_ref, lfT_ref, cT_ref, carry_ref):
    ti = pl.program_id(1)
    h = _rms_bf16(x_ref[0], g_ref[...])
    q = jnp.dot(h, wq_ref[...], preferred_element_type=F32)
    q_ref[0] = (q * SCALE).astype(BF16)
    kv = lax.dot_general(wkvT_ref[...], h, _NT, preferred_element_type=F32)
    nk = kT_ref.shape[1] * HEAD_DIM
    k = kv[:nk].reshape(kT_ref.shape[1:])
    v = kv[nk:].reshape(vT_ref.shape[1:])
    kT_ref[0] = k
    vT_ref[0] = v
    kTb_ref[0] = k.astype(BF16)
    vTb_ref[0] = v.astype(BF16)
    f = lax.dot_general(wfT_ref[...], h, _NT, preferred_element_type=F32)
    lf = _log_sigmoid(f + bf_ref[...])
    lfT_ref[0] = lf

    @pl.when(ti == 0)
    def _():
        carry_ref[...] = jnp.zeros_like(carry_ref)

    hi, mid, lo = _split3(lf)
    tri = tri_ref[...]
    c = (jnp.dot(hi, tri, preferred_element_type=F32) + jnp.dot(mid, tri, preferred_element_type=F32)
         + jnp.dot(lo, tri, preferred_element_type=F32)) + carry_ref[:, 0:1]
    cT_ref[0] = c
    carry_ref[...] = jnp.broadcast_to(c[:, -1:], carry_ref.shape)


def _fox_proj(x, g, wq, wkvT, wfT, bf, tm):
    B, S, D = x.shape
    tm = min(tm, S)
    nt = S // tm
    kvshape = (B, N_KV_FOX, HEAD_DIM, S)
    kvblock = pl.BlockSpec((1, N_KV_FOX, HEAD_DIM, tm), lambda b, t: (b, 0, 0, t))
    hblock = pl.BlockSpec((1, N_HEADS, tm), lambda b, t: (b, 0, t))
    const = lambda shape: pl.BlockSpec(shape, lambda b, t: (0,) * len(shape))
    return pl.pallas_call(
        _fox_proj_kernel,
        grid=(B, nt),
        in_specs=[pl.BlockSpec((1, tm, D), lambda b, t: (b, t, 0)), const(g.shape), const(wq.shape),
                  const(wkvT.shape), const(wfT.shape), const(bf.shape), const((tm, tm))],
        out_specs=[pl.BlockSpec((1, tm, N_HEADS * HEAD_DIM), lambda b, t: (b, t, 0)),
                   kvblock, kvblock, kvblock, kvblock, hblock, hblock],
        out_shape=[jax.ShapeDtypeStruct((B, S, N_HEADS * HEAD_DIM), BF16),
                   jax.ShapeDtypeStruct(kvshape, F32), jax.ShapeDtypeStruct(kvshape, F32),
                   jax.ShapeDtypeStruct(kvshape, BF16), jax.ShapeDtypeStruct(kvshape, BF16),
                   jax.ShapeDtypeStruct((B, N_HEADS, S), F32), jax.ShapeDtypeStruct((B, N_HEADS, S), F32)],
        scratch_shapes=[pltpu.VMEM((N_HEADS, LANES), F32)],
        compiler_params=_cparams("arbitrary", "arbitrary"),
        name="fox_proj",
    )(x, g, wq, wkvT, wfT, bf, _upper_tri(tm))


def _fox_prompt_attn_kernel(q_ref, kT_ref, vT_ref, c_ref, o_ref, m_ref, l_ref, acc_ref, *, tq, tk):
    qi = pl.program_id(2)
    n_diag = tq // tk
    row = lax.broadcasted_iota(jnp.int32, (tq, tk), 0)
    col = lax.broadcasted_iota(jnp.int32, (tq, tk), 1)

    for g in range(N_HEADS // N_KV_FOX):
        q = q_ref[0, :, g * HEAD_DIM:(g + 1) * HEAD_DIM]
        m_ref[...] = jnp.full_like(m_ref, NEG_INF)
        l_ref[...] = jnp.zeros_like(l_ref)
        acc_ref[...] = jnp.zeros_like(acc_ref)

        def chunk(k0, diag_j):
            kT = kT_ref[0, 0, :, pl.ds(k0, tk)]
            vT = vT_ref[0, 0, :, pl.ds(k0, tk)]
            s = jnp.dot(q, kT, preferred_element_type=F32) - c_ref[0, 0, g:g + 1, pl.ds(k0, tk)]
            if diag_j is not None:
                s = jnp.where(col + diag_j * tk <= row, s, NEG_INF)
            m_prev = m_ref[...]
            m_new = jnp.maximum(m_prev, jnp.max(s, axis=-1, keepdims=True))
            p = jnp.exp(s - m_new)
            alpha = jnp.exp(m_prev - m_new)
            l_ref[...] = alpha * l_ref[...] + jnp.sum(p, axis=-1, keepdims=True)
            acc_ref[...] = alpha * acc_ref[...] + lax.dot_general(
                p.astype(BF16), vT, _NT, preferred_element_type=F32)
            m_ref[...] = m_new

        def body(i, carry):
            chunk(pl.multiple_of(i * tk, tk), None)
            return carry

        lax.fori_loop(0, qi * n_diag, body, 0)
        for j in range(n_diag):
            chunk(pl.multiple_of(qi * tq + j * tk, tk), j)
        o_ref[0, :, g * HEAD_DIM:(g + 1) * HEAD_DIM] = (acc_ref[...] / l_ref[...]).astype(BF16)


def _fox_prompt_attn(q, kTb, vTb, cT, tq, tk):
    B, S, _ = q.shape
    tq = min(tq, S)
    tk = min(tk, tq)
    G = N_HEADS // N_KV_FOX
    c4 = cT.reshape(B, N_KV_FOX, G, S)
    kvspec = pl.BlockSpec((1, 1, HEAD_DIM, S), lambda b, h, i: (b, h, 0, 0))
    return pl.pallas_call(
        functools.partial(_fox_prompt_attn_kernel, tq=tq, tk=tk),
        grid=(B, N_KV_FOX, S // tq),
        in_specs=[pl.BlockSpec((1, tq, G * HEAD_DIM), lambda b, h, i: (b, i, h)), kvspec, kvspec,
                  pl.BlockSpec((1, 1, G, S), lambda b, h, i: (b, h, 0, 0))],
        out_specs=pl.BlockSpec((1, tq, G * HEAD_DIM), lambda b, h, i: (b, i, h)),
        out_shape=jax.ShapeDtypeStruct((B, S, N_HEADS * HEAD_DIM), BF16),
        scratch_shapes=[pltpu.VMEM((tq, 1), F32), pltpu.VMEM((tq, 1), F32), pltpu.VMEM((tq, HEAD_DIM), F32)],
        compiler_params=_cparams("arbitrary", "arbitrary", "arbitrary"),
        name="fox_prompt_attn",
    )(q, kTb, vTb, c4)


def _fox_decode_kernel(pt_ref, q_ref, kn_ref, vn_ref, lfn_ref, tri_ref, kc_hbm, vc_hbm, lc_hbm,
                       o_ref, kbuf, vbuf, lbuf, sems, qbd_ref, m_ref, l_ref, acc_ref, carry_ref,
                       *, pb, nblk, page_base):
    s_idx = pl.program_id(0)
    j = pl.program_id(1)
    t = s_idx * nblk + j
    n_steps = pl.num_programs(0) * nblk
    slot = lax.rem(t, 2)
    page = LANES
    nh = N_HEADS

    def copies(seq, blk, sl):
        out = []
        for i in range(pb):
            pid = pt_ref[seq, blk * pb + i] + page_base
            out.append(pltpu.make_async_copy(kc_hbm.at[pid], kbuf.at[sl, :, pl.ds(i * page, page)], sems.at[sl, 0]))
            out.append(pltpu.make_async_copy(vc_hbm.at[pid], vbuf.at[sl, :, pl.ds(i * page, page)], sems.at[sl, 1]))
            out.append(pltpu.make_async_copy(lc_hbm.at[pid], lbuf.at[sl, pl.ds(i * nh, nh), :], sems.at[sl, 2]))
        return out

    @pl.when(t == 0)
    def _():
        for cp in copies(s_idx, j, slot):
            cp.start()

    @pl.when(t + 1 < n_steps)
    def _():
        last = j + 1 == nblk
        for cp in copies(jnp.where(last, s_idx + 1, s_idx), jnp.where(last, 0, j + 1), 1 - slot):
            cp.start()

    @pl.when(j == 0)
    def _():
        q16 = q_ref[0].astype(F32)
        qt = jnp.concatenate([q16] * N_KV_FOX, axis=1)
        r = lax.broadcasted_iota(jnp.int32, qt.shape, 0) // (N_HEADS // N_KV_FOX)
        c = lax.broadcasted_iota(jnp.int32, qt.shape, 1) // HEAD_DIM
        qbd_ref[...] = jnp.where(r == c, qt, 0.0).astype(BF16)
        m_ref[...] = jnp.full_like(m_ref, NEG_INF)
        l_ref[...] = jnp.zeros_like(l_ref)
        acc_ref[...] = jnp.zeros_like(acc_ref)
        carry_ref[...] = jnp.zeros_like(carry_ref)

    for cp in copies(s_idx, j, slot):
        cp.wait()

    qbd = qbd_ref[...]
    x = lbuf[slot]
    n = pb * nh
    w3 = jnp.dot(jnp.concatenate(_split3(x), axis=0), tri_ref[...], preferred_element_type=F32)
    w = w3[:n] + w3[n:2 * n] + w3[2 * n:]
    off = carry_ref[:, 0:1]
    cs = []
    for i in range(pb):
        wi = w[i * nh:(i + 1) * nh]
        cs.append(wi + off)
        off = off + wi[:, page - 1:page]
    carry_ref[...] = jnp.broadcast_to(off, carry_ref.shape)
    bias = jnp.concatenate(cs, axis=1)

    s = jnp.dot(qbd, kbuf[slot].astype(BF16), preferred_element_type=F32) - bias
    m_prev = m_ref[...]
    m_new = jnp.maximum(m_prev, jnp.max(s, axis=-1, keepdims=True))
    p = jnp.exp(s - m_new)
    alpha = jnp.exp(m_prev - m_new)
    l_ref[...] = alpha * l_ref[...] + jnp.sum(p, axis=-1, keepdims=True)
    acc_ref[...] = alpha * acc_ref[...] + lax.dot_general(
        p.astype(BF16), vbuf[slot].astype(BF16), _NT, preferred_element_type=F32)
    m_ref[...] = m_new

    @pl.when(j == nblk - 1)
    def _():
        kn = kn_ref[0].astype(BF16).astype(F32)
        s_n = jnp.sum(qbd.astype(F32) * kn, axis=-1, keepdims=True) - (off + lfn_ref[0])
        m_p = m_ref[...]
        m_f = jnp.maximum(m_p, s_n)
        a = jnp.exp(m_p - m_f)
        p_n = jnp.exp(s_n - m_f)
        l_f = a * l_ref[...] + p_n
        acc = (a * acc_ref[...] + p_n * vn_ref[0].astype(BF16).astype(F32)) / l_f
        rr = lax.broadcasted_iota(jnp.int32, (nh, HEAD_DIM), 0) // (N_HEADS // N_KV_FOX)
        o = jnp.zeros((nh, HEAD_DIM), F32)
        for hh in range(N_KV_FOX):
            o = jnp.where(rr == hh, acc[:, hh * HEAD_DIM:(hh + 1) * HEAD_DIM], o)
        o_ref[0] = o.astype(BF16)


def _fox_decode_attn(page_table, q, k_new, v_new, lf_new, kc, vc, lc, page_base, pb):
    N, npages = page_table.shape
    pb = math.gcd(pb, npages)
    nblk = npages // pb
    page = kc.shape[-1]
    assert page == LANES and lc.shape[-1] == LANES
    khd = N_KV_FOX * HEAD_DIM
    grid_spec = pltpu.PrefetchScalarGridSpec(
        num_scalar_prefetch=1,
        grid=(N, nblk),
        in_specs=[pl.BlockSpec((1, N_HEADS, HEAD_DIM), lambda s, j, pt: (s, 0, 0)),
                  pl.BlockSpec((1, 1, khd), lambda s, j, pt: (s, 0, 0)),
                  pl.BlockSpec((1, 1, khd), lambda s, j, pt: (s, 0, 0)),
                  pl.BlockSpec((1, N_HEADS, 1), lambda s, j, pt: (s, 0, 0)),
                  pl.BlockSpec((page, page), lambda s, j, pt: (0, 0)),
                  pl.BlockSpec(memory_space=pl.ANY), pl.BlockSpec(memory_space=pl.ANY),
                  pl.BlockSpec(memory_space=pl.ANY)],
        out_specs=pl.BlockSpec((1, N_HEADS, HEAD_DIM), lambda s, j, pt: (s, 0, 0)),
        scratch_shapes=[pltpu.VMEM((2, khd, pb * page), F32), pltpu.VMEM((2, khd, pb * page), F32),
                        pltpu.VMEM((2, pb * N_HEADS, page), F32), pltpu.SemaphoreType.DMA((2, 3)),
                        pltpu.VMEM((N_HEADS, khd), BF16), pltpu.VMEM((N_HEADS, 1), F32),
                        pltpu.VMEM((N_HEADS, 1), F32), pltpu.VMEM((N_HEADS, khd), F32),
                        pltpu.VMEM((N_HEADS, LANES), F32)],
    )
    return pl.pallas_call(
        functools.partial(_fox_decode_kernel, pb=pb, nblk=nblk, page_base=page_base),
        grid_spec=grid_spec,
        out_shape=jax.ShapeDtypeStruct((N, N_HEADS, HEAD_DIM), BF16),
        compiler_params=_cparams("arbitrary", "arbitrary"),
        name="fox_decode_attn",
    )(page_table, q, k_new, v_new, lf_new, _upper_tri(page), kc, vc, lc)


def _rope_rows(x, cos, sin_signed):
    n = x.shape[1]
    half = HEAD_DIM // 2
    lane = lax.broadcasted_iota(jnp.int32, x.shape, 1)
    rot = jnp.where(lane % HEAD_DIM < half, pltpu.roll(x, n - half, 1), pltpu.roll(x, half, 1))
    reps = n // LANES
    return x * jnp.concatenate([cos] * reps, axis=1) + rot * jnp.concatenate([sin_signed] * reps, axis=1)


def _rope_cols(xT, cosT, sinT):
    half = HEAD_DIM // 2
    out = []
    for h in range(xT.shape[0] // HEAD_DIM):
        x1 = xT[h * HEAD_DIM:h * HEAD_DIM + half]
        x2 = xT[h * HEAD_DIM + half:(h + 1) * HEAD_DIM]
        out += [x1 * cosT - x2 * sinT, x2 * cosT + x1 * sinT]
    return jnp.concatenate(out, axis=0)


def _swa_proj_kernel(x_ref, g_ref, wq_ref, wkvT_ref, wkv_ref, cos_ref, sin_ref, cosT_ref, sinT_ref,
                     q_ref, kT_ref, vT_ref, kTb_ref, vTb_ref, kn_ref, vn_ref):
    h = _rms_bf16(x_ref[0], g_ref[...])
    q = jnp.dot(h, wq_ref[...], preferred_element_type=F32)
    q_ref[0] = (_rope_rows(q, cos_ref[...], sin_ref[...]) * SCALE).astype(BF16)
    kv = lax.dot_general(wkvT_ref[...], h, _NT, preferred_element_type=F32)
    nk = N_KV_SWA * HEAD_DIM
    k = _rope_cols(kv[:nk], cosT_ref[...], sinT_ref[...])
    v = kv[nk:]
    kT_ref[0] = k
    vT_ref[0] = v
    kTb_ref[0] = k.astype(BF16)
    vTb_ref[0] = v.astype(BF16)
    kvn = jnp.dot(h, wkv_ref[...], preferred_element_type=F32)
    kn_ref[0] = _rope_rows(kvn[:, :nk], cos_ref[...], sin_ref[...])
    vn_ref[0] = kvn[:, nk:]


def _swa_proj(x, g, wq, wkvT, wkv, pos, tm):
    B, S, D = x.shape
    tm = min(tm, S)
    half = HEAD_DIM // 2
    inv = ROPE_THETA ** (-jnp.arange(half, dtype=F32) / half)
    ang = pos.astype(F32)[:, None] * inv[None, :]
    cos, sin = jnp.cos(ang), jnp.sin(ang)
    cos_rows = jnp.concatenate([cos] * (LANES // half), axis=1)
    sin_rows = jnp.concatenate([-sin, sin] * (LANES // HEAD_DIM), axis=1)
    nk = N_KV_SWA * HEAD_DIM
    const = lambda shape: pl.BlockSpec(shape, lambda b, t: (0,) * len(shape))
    tspec = pl.BlockSpec((1, nk, tm), lambda b, t: (b, 0, t))
    rspec = pl.BlockSpec((1, tm, nk), lambda b, t: (b, t, 0))
    return pl.pallas_call(
        _swa_proj_kernel,
        grid=(B, S // tm),
        in_specs=[pl.BlockSpec((1, tm, D), lambda b, t: (b, t, 0)), const(g.shape), const(wq.shape),
                  const(wkvT.shape), const(wkv.shape),
                  pl.BlockSpec((tm, LANES), lambda b, t: (t, 0)), pl.BlockSpec((tm, LANES), lambda b, t: (t, 0)),
                  pl.BlockSpec((half, tm), lambda b, t: (0, t)), pl.BlockSpec((half, tm), lambda b, t: (0, t))],
        out_specs=[pl.BlockSpec((1, tm, N_HEADS * HEAD_DIM), lambda b, t: (b, t, 0)),
                   tspec, tspec, tspec, tspec, rspec, rspec],
        out_shape=[jax.ShapeDtypeStruct((B, S, N_HEADS * HEAD_DIM), BF16),
                   jax.ShapeDtypeStruct((B, nk, S), F32), jax.ShapeDtypeStruct((B, nk, S), F32),
                   jax.ShapeDtypeStruct((B, nk, S), BF16), jax.ShapeDtypeStruct((B, nk, S), BF16),
                   jax.ShapeDtypeStruct((B, S, nk), F32), jax.ShapeDtypeStruct((B, S, nk), F32)],
        compiler_params=_cparams("arbitrary", "arbitrary"),
        name="swa_proj",
    )(x, g, wq, wkvT, wkv, cos_rows, sin_rows, cos.T, sin.T)


def _swa_prompt_attn_kernel(sink_ref, q_ref, kT_ref, vT_ref, o_ref, *, tq):
    qi = pl.program_id(1)
    G = N_HEADS // N_KV_SWA
    W = WINDOW
    r = lax.broadcasted_iota(jnp.int32, (G * W, 2 * W), 0) % W
    c = lax.broadcasted_iota(jnp.int32, (G * W, 2 * W), 1)
    for n in range(tq // W):
        blk = qi * (tq // W) + n
        k0 = pl.multiple_of(jnp.maximum(blk - 1, 0) * W, W)
        delta = (blk * W + r) - (k0 + c)
        mask = (delta >= 0) & (delta <= W)
        outs = []
        for kv in range(N_KV_SWA):
            q8 = jnp.concatenate(
                [q_ref[0, n * W:(n + 1) * W, (kv * G + g) * HEAD_DIM:(kv * G + g + 1) * HEAD_DIM]
                 for g in range(G)], axis=0)
            sink = jnp.concatenate(
                [jnp.full((W, 1), sink_ref[kv * G + g], F32) for g in range(G)], axis=0)
            kT = kT_ref[0, kv * HEAD_DIM:(kv + 1) * HEAD_DIM, pl.ds(k0, 2 * W)]
            vT = vT_ref[0, kv * HEAD_DIM:(kv + 1) * HEAD_DIM, pl.ds(k0, 2 * W)]
            s = jnp.where(mask, jnp.dot(q8, kT, preferred_element_type=F32), NEG_INF)
            m = jnp.maximum(jnp.max(s, axis=-1, keepdims=True), sink)
            p = jnp.exp(s - m)
            p = p / (jnp.sum(p, axis=-1, keepdims=True) + jnp.exp(sink - m))
            o8 = lax.dot_general(p.astype(BF16), vT, _NT, preferred_element_type=F32)
            outs += [o8[g * W:(g + 1) * W] for g in range(G)]
        o_ref[0, n * W:(n + 1) * W, :] = jnp.concatenate(outs, axis=1).astype(BF16)


def _swa_prompt_attn(sinks, q, kTb, vTb, tq):
    B, S, _ = q.shape
    assert S % WINDOW == 0 and S >= 2 * WINDOW
    tq = min(tq, S)
    nk = N_KV_SWA * HEAD_DIM
    grid_spec = pltpu.PrefetchScalarGridSpec(
        num_scalar_prefetch=1,
        grid=(B, S // tq),
        in_specs=[pl.BlockSpec((1, tq, N_HEADS * HEAD_DIM), lambda b, i, sk: (b, i, 0)),
                  pl.BlockSpec((1, nk, S), lambda b, i, sk: (b, 0, 0)),
                  pl.BlockSpec((1, nk, S), lambda b, i, sk: (b, 0, 0))],
        out_specs=pl.BlockSpec((1, tq, N_HEADS * HEAD_DIM), lambda b, i, sk: (b, i, 0)),
    )
    return pl.pallas_call(
        functools.partial(_swa_prompt_attn_kernel, tq=tq),
        grid_spec=grid_spec,
        out_shape=jax.ShapeDtypeStruct((B, S, N_HEADS * HEAD_DIM), BF16),
        compiler_params=_cparams("arbitrary", "arbitrary"),
        name="swa_prompt_attn",
    )(sinks, q, kTb, vTb)


def _swa_decode_kernel(q_ref, kc_ref, vc_ref, kn_ref, vn_ref, knT_ref, vnT_ref, sink_ref,
                       o_ref, ko_ref, vo_ref):
    s_idx = pl.program_id(0)
    G = N_HEADS // N_KV_SWA
    nk = N_KV_SWA * HEAD_DIM
    wb = kc_ref.shape[2]
    q16 = q_ref[0].astype(F32)
    qt = jnp.concatenate([q16] * N_KV_SWA, axis=1)
    rr = lax.broadcasted_iota(jnp.int32, qt.shape, 0) // G
    cc = lax.broadcasted_iota(jnp.int32, qt.shape, 1) // HEAD_DIM
    qbd = jnp.where(rr == cc, qt, 0.0)
    kT = kc_ref[0]
    vT = vc_ref[0]
    sink = sink_ref[...]
    s_p = jnp.dot(qbd.astype(BF16), kT.astype(BF16), preferred_element_type=F32)
    s_n = jnp.sum(qbd.astype(BF16).astype(F32) * kn_ref[0].astype(BF16).astype(F32), axis=-1, keepdims=True)
    m = jnp.maximum(jnp.maximum(jnp.max(s_p, axis=-1, keepdims=True), s_n), sink)
    p_p = jnp.exp(s_p - m)
    p_n = jnp.exp(s_n - m)
    den = jnp.sum(p_p, axis=-1, keepdims=True) + p_n + jnp.exp(sink - m)
    acc = lax.dot_general((p_p / den).astype(BF16), vT.astype(BF16), _NT, preferred_element_type=F32)
    acc = acc + (p_n / den).astype(BF16).astype(F32) * vn_ref[0].astype(BF16).astype(F32)
    rh = lax.broadcasted_iota(jnp.int32, (N_HEADS, HEAD_DIM), 0) // G
    o = jnp.zeros((N_HEADS, HEAD_DIM), F32)
    for kv in range(N_KV_SWA):
        o = jnp.where(rh == kv, acc[:, kv * HEAD_DIM:(kv + 1) * HEAD_DIM], o)
    o_ref[0] = o.astype(BF16)
    lane = lax.broadcasted_iota(jnp.int32, (nk, wb), 1)
    seq_lane = lax.broadcasted_iota(jnp.int32, knT_ref.shape, 1)
    k_col = jnp.sum(jnp.where(seq_lane == s_idx, knT_ref[...], 0.0), axis=-1, keepdims=True)
    v_col = jnp.sum(jnp.where(seq_lane == s_idx, vnT_ref[...], 0.0), axis=-1, keepdims=True)
    ko_ref[0] = jnp.where(lane == wb - 1, k_col, pltpu.roll(kT, wb - 1, 1))
    vo_ref[0] = jnp.where(lane == wb - 1, v_col, pltpu.roll(vT, wb - 1, 1))


def _swa_decode_attn(q, kc, vc, k_new, v_new, k_newT, v_newT, sinks):
    N, nk, wb = kc.shape
    assert wb == LANES and wb <= WINDOW
    seq3 = lambda shape: pl.BlockSpec((1,) + shape, lambda s: (s, 0, 0))
    const = lambda shape: pl.BlockSpec(shape, lambda s: (0,) * len(shape))
    return pl.pallas_call(
        _swa_decode_kernel,
        grid=(N,),
        in_specs=[seq3((N_HEADS, HEAD_DIM)), seq3((nk, wb)), seq3((nk, wb)), seq3((1, nk)), seq3((1, nk)),
                  const(k_newT.shape), const(v_newT.shape), const(sinks.shape)],
        out_specs=[seq3((N_HEADS, HEAD_DIM)), seq3((nk, wb)), seq3((nk, wb))],
        out_shape=[jax.ShapeDtypeStruct((N, N_HEADS, HEAD_DIM), BF16),
                   jax.ShapeDtypeStruct((N, nk, wb), F32), jax.ShapeDtypeStruct((N, nk, wb), F32)],
        compiler_params=_cparams("arbitrary"),
        name="swa_decode_attn",
    )(q, kc, vc, k_new, v_new, k_newT, v_newT, sinks)


def _silu(x):
    return x * (1.0 / (1.0 + jnp.exp(-x)))


def _post_prompt_kernel(x_ref, o_ref, wo_ref, g_ref, win_ref, cw_ref, cb_ref, wout_ref, gf_ref,
                        y_ref, ulast_ref, halo_ref, ubuf_ref, *, n_chunks, final_norm):
    ti = pl.program_id(1)
    tm = x_ref.shape[1]
    F = wout_ref.shape[0]
    fc = F // n_chunks
    pad = ubuf_ref.shape[0] - tm

    @pl.when(ti == 0)
    def _():
        halo_ref[...] = jnp.zeros_like(halo_ref)

    xn = x_ref[0] + jnp.dot(o_ref[0], wo_ref[...], preferred_element_type=F32)
    h = _rms_bf16(xn, g_ref[...])
    y = jnp.zeros_like(xn)
    for ci in range(n_chunks):
        halves = []
        for c0 in (ci * fc, F + ci * fc):
            u = jnp.dot(h, win_ref[:, c0:c0 + fc], preferred_element_type=F32)
            ubuf_ref[0:pad, :] = halo_ref[:, c0:c0 + fc]
            ubuf_ref[pad:pad + tm, :] = u
            halo_ref[:, c0:c0 + fc] = u[tm - pad:]
            ulast_ref[0, :, c0:c0 + fc] = u[tm - pad:]
            conv = (cb_ref[:, c0:c0 + fc] + cw_ref[0:1, c0:c0 + fc] * ubuf_ref[pad - 2:pad - 2 + tm, :]
                    + cw_ref[1:2, c0:c0 + fc] * ubuf_ref[pad - 1:pad - 1 + tm, :]
                    + cw_ref[2:3, c0:c0 + fc] * u)
            halves.append(conv)
        a = (_silu(halves[0]) * halves[1]).astype(BF16)
        y = y + jnp.dot(a, wout_ref[ci * fc:(ci + 1) * fc, :], preferred_element_type=F32)
    out = xn + y
    if final_norm:
        out = out * lax.rsqrt(jnp.mean(out * out, axis=-1, keepdims=True) + RMS_EPS) * gf_ref[...]
    y_ref[0] = out


def _post_prompt(x, o, wo, g, win, cw, cb, wout, gf, final_norm, tm, n_chunks):
    B, S, D = x.shape
    tm = min(tm, S)
    F2 = win.shape[1]
    pad = 8
    const = lambda shape: pl.BlockSpec(shape, lambda b, t: (0,) * len(shape))
    row = pl.BlockSpec((1, tm, D), lambda b, t: (b, t, 0))
    return pl.pallas_call(
        functools.partial(_post_prompt_kernel, n_chunks=n_chunks, final_norm=final_norm),
        grid=(B, S // tm),
        in_specs=[row, row, const(wo.shape), const(g.shape), const(win.shape), const(cw.shape),
                  const(cb.shape), const(wout.shape), const(gf.shape)],
        out_specs=[row, pl.BlockSpec((1, pad, F2), lambda b, t: (b, 0, 0))],
        out_shape=[jax.ShapeDtypeStruct((B, S, D), F32), jax.ShapeDtypeStruct((B, pad, F2), F32)],
        scratch_shapes=[pltpu.VMEM((pad, F2), F32), pltpu.VMEM((tm + pad, F2 // (2 * n_chunks)), F32)],
        compiler_params=_cparams("arbitrary", "arbitrary"),
        name="post_prompt",
    )(x, o, wo, g, win, cw, cb, wout, gf)


def _post_sample_kernel(x_ref, o_ref, wo_ref, g_ref, win_ref, cw_ref, cb_ref, wout_ref, gf_ref,
                        s0_ref, s1_ref, y_ref, u_ref, *, n_chunks, final_norm):
    F = wout_ref.shape[0]
    fc = F // n_chunks
    xn = x_ref[...] + jnp.dot(o_ref[...], wo_ref[...], preferred_element_type=F32)
    h = _rms_bf16(xn, g_ref[...])
    y = jnp.zeros_like(xn)
    for ci in range(n_chunks):
        halves = []
        for c0 in (ci * fc, F + ci * fc):
            u = jnp.dot(h, win_ref[:, c0:c0 + fc], preferred_element_type=F32)
            u_ref[:, c0:c0 + fc] = u
            halves.append(cb_ref[:, c0:c0 + fc] + cw_ref[0:1, c0:c0 + fc] * s0_ref[:, c0:c0 + fc]
                          + cw_ref[1:2, c0:c0 + fc] * s1_ref[:, c0:c0 + fc] + cw_ref[2:3, c0:c0 + fc] * u)
        a = (_silu(halves[0]) * halves[1]).astype(BF16)
        y = y + jnp.dot(a, wout_ref[ci * fc:(ci + 1) * fc, :], preferred_element_type=F32)
    out = xn + y
    if final_norm:
        out = out * lax.rsqrt(jnp.mean(out * out, axis=-1, keepdims=True) + RMS_EPS) * gf_ref[...]
    y_ref[...] = out


def _post_sample(x, o, wo, g, win, cw, cb, wout, gf, s0, s1, final_norm, n_chunks):
    N, D = x.shape
    F2 = win.shape[1]
    return pl.pallas_call(
        functools.partial(_post_sample_kernel, n_chunks=n_chunks, final_norm=final_norm),
        out_shape=[jax.ShapeDtypeStruct((N, D), F32), jax.ShapeDtypeStruct((N, F2), F32)],
        compiler_params=_cparams(),
        name="post_sample",
    )(x, o, wo, g, win, cw, cb, wout, gf, s0, s1)


PROJ_TILE = 512
POST_TILE = 512
FFN_CHUNKS = 2
FOX_TQ, FOX_TK = 1024, 512
SWA_TQ = 512
DECODE_PAGES_PER_BLOCK = 16


def kernel(x_prompt, x_sample, cache_fox_k, cache_fox_v, cache_fox_logf, cache_swa_k, cache_swa_v, state_ffn_conv, page_table, norm_mixer, norm_ffn, norm_final, fox_w_qkv, fox_b_f, fox_w_o, swa_w_qkv, swa_sinks, swa_w_o, ffn_w_in, ffn_conv_w, ffn_conv_b, ffn_w_out):
    B, S, D = x_prompt.shape
    N = x_sample.shape[0]
    assert x_sample.shape[1] == 1
    depth = norm_mixer.shape[0]
    n_phys, page = cache_fox_k.shape[1], cache_fox_k.shape[2]
    past_len = page_table.shape[1] * page
    nq = N_HEADS * HEAD_DIM
    nkf = N_KV_FOX * HEAD_DIM
    nks = N_KV_SWA * HEAD_DIM
    F = ffn_w_out.shape[1]

    xp = x_prompt
    xs = x_sample.reshape(1, N, D)
    gfin = norm_final.reshape(1, D)

    kc = jnp.transpose(cache_fox_k, (0, 1, 3, 4, 2)).reshape(-1, nkf, page)
    vc = jnp.transpose(cache_fox_v, (0, 1, 3, 4, 2)).reshape(-1, nkf, page)
    lc = jnp.transpose(cache_fox_logf, (0, 1, 3, 2)).reshape(-1, N_HEADS, page)

    fkp, fvp, flp, fks, fvs, fls = [], [], [], [], [], []
    skp, svp, sks, svs = [], [], [], []
    cps, css = [], []
    for i in range(depth):
        j = i // 2
        g_mix = norm_mixer[i].reshape(1, D)
        if i % 2 == 0:
            w = fox_w_qkv[j]
            wq = w[:, :nq].astype(BF16)
            wkvT = w[:, nq:nq + 2 * nkf].T.astype(BF16)
            wkv = w[:, nq:nq + 2 * nkf].astype(BF16)
            wfT = w[:, nq + 2 * nkf:].T.astype(BF16)
            bf = fox_b_f[j].reshape(N_HEADS, 1)
            q, kT, vT, kTb, vTb, lfT, cT = _fox_proj(xp, g_mix, wq, wkvT, wfT, bf, PROJ_TILE)
            op = _fox_prompt_attn(q, kTb, vTb, cT, FOX_TQ, FOX_TK)
            fkp.append(jnp.transpose(kT, (0, 3, 1, 2)))
            fvp.append(jnp.transpose(vT, (0, 3, 1, 2)))
            flp.append(jnp.transpose(lfT, (0, 2, 1)))

            qs, kTs, vTs, _, _, lfTs, _ = _fox_proj(xs, g_mix, wq, wkvT, wfT, bf, PROJ_TILE)
            k_new = jnp.transpose(kTs[0], (2, 0, 1)).reshape(N, 1, nkf)
            v_new = jnp.transpose(vTs[0], (2, 0, 1)).reshape(N, 1, nkf)
            lf_new = jnp.transpose(lfTs[0], (1, 0)).reshape(N, N_HEADS, 1)
            os_ = _fox_decode_attn(page_table, qs.reshape(N, N_HEADS, HEAD_DIM), k_new, v_new, lf_new,
                                   kc, vc, lc, j * n_phys, DECODE_PAGES_PER_BLOCK)
            os_ = os_.reshape(N, nq)
            fks.append(jnp.transpose(kTs[0], (2, 0, 1)).reshape(N, 1, N_KV_FOX, HEAD_DIM))
            fvs.append(jnp.transpose(vTs[0], (2, 0, 1)).reshape(N, 1, N_KV_FOX, HEAD_DIM))
            fls.append(jnp.transpose(lfTs[0], (1, 0)).reshape(N, 1, N_HEADS))
            wo = fox_w_o[j].astype(BF16)
        else:
            w = swa_w_qkv[j]
            wq = w[:, :nq].astype(BF16)
            wkv = w[:, nq:].astype(BF16)
            wkvT = w[:, nq:].T.astype(BF16)
            sinks = swa_sinks[j]
            q, kT, vT, kTb, vTb, _, _ = _swa_proj(xp, g_mix, wq, wkvT, wkv, jnp.arange(S), PROJ_TILE)
            op = _swa_prompt_attn(sinks, q, kTb, vTb, SWA_TQ)
            keep = min(WINDOW, S)
            skp.append(jnp.transpose(kT[:, :, S - keep:].reshape(B, N_KV_SWA, HEAD_DIM, keep), (0, 3, 1, 2)))
            svp.append(jnp.transpose(vT[:, :, S - keep:].reshape(B, N_KV_SWA, HEAD_DIM, keep), (0, 3, 1, 2)))

            pos_s = jnp.full((N,), past_len, jnp.int32)
            qs, kTs, vTs, _, _, kns, vns = _swa_proj(xs, g_mix, wq, wkvT, wkv, pos_s, PROJ_TILE)
            wb = cache_swa_k.shape[2]
            kcs = jnp.transpose(cache_swa_k[j], (0, 2, 3, 1)).reshape(N, nks, wb)
            vcs = jnp.transpose(cache_swa_v[j], (0, 2, 3, 1)).reshape(N, nks, wb)
            os_, ko, vo = _swa_decode_attn(qs.reshape(N, N_HEADS, HEAD_DIM), kcs, vcs,
                                           kns.reshape(N, 1, nks), vns.reshape(N, 1, nks),
                                           kTs[0], vTs[0], sinks.reshape(N_HEADS, 1))
            os_ = os_.reshape(N, nq)
            sks.append(jnp.transpose(ko.reshape(N, N_KV_SWA, HEAD_DIM, wb), (0, 3, 1, 2)))
            svs.append(jnp.transpose(vo.reshape(N, N_KV_SWA, HEAD_DIM, wb), (0, 3, 1, 2)))
            wo = swa_w_o[j].astype(BF16)

        g_ffn = norm_ffn[i].reshape(1, D)
        win = ffn_w_in[i].astype(BF16)
        wout = ffn_w_out[i].astype(BF16)
        cw = ffn_conv_w[i]
        cb = ffn_conv_b[i].reshape(1, 2 * F)
        last = i == depth - 1
        xp, ulast = _post_prompt(xp, op, wo, g_ffn, win, cw, cb, wout, gfin, last, POST_TILE, FFN_CHUNKS)
        cps.append(ulast[:, -(CONV_W - 1):])
        st = state_ffn_conv[i]
        xs2, us = _post_sample(xs[0], os_, wo, g_ffn, win, cw, cb, wout, gfin, st[:, 0], st[:, 1],
                               last, FFN_CHUNKS)
        xs = xs2.reshape(1, N, D)
        css.append(jnp.stack([st[:, 1], us], axis=1))

    return (xp, xs.reshape(N, 1, D),
            jnp.stack(fkp), jnp.stack(fvp), jnp.stack(flp),
            jnp.stack(fks), jnp.stack(fvs), jnp.stack(fls),
            jnp.stack(skp), jnp.stack(svp), jnp.stack(sks), jnp.stack(svs),
            jnp.stack(cps), jnp.stack(css))
```

```python
import functools
import math

import jax
import jax.numpy as jnp
from jax import lax
from jax.experimental import pallas as pl
from jax.experimental.pallas import tpu as pltpu

HEAD_DIM = 64
N_HEADS = 16
N_KV_FOX = 8
N_KV_SWA = 2
WINDOW = 128
ROPE_THETA = 10000.0
RMS_EPS = 1e-6
CONV_W = 3
SCALE = HEAD_DIM ** -0.5
LOG2E = math.log2(math.e)

F32 = jnp.float32
BF16 = jnp.bfloat16

V7X_VMEM_LIMIT_BYTES = 56 * 1024 * 1024
LANES = 128
NEG_INF = float("-inf")

_NT = (((1,), (1,)), ((), ()))


def _cparams(*sem):
    return pltpu.CompilerParams(dimension_semantics=sem, vmem_limit_bytes=V7X_VMEM_LIMIT_BYTES)


def _rms_bf16(x, g):
    y = x * lax.rsqrt(jnp.mean(x * x, axis=-1, keepdims=True) + RMS_EPS)
    return (y * g).astype(BF16)


def _split3(x):
    hi = x.astype(BF16)
    r1 = x - hi.astype(F32)
    mid = r1.astype(BF16)
    lo = (r1 - mid.astype(F32)).astype(BF16)
    return hi, mid, lo


def _log_sigmoid(x):
    return jnp.minimum(x, 0.0) - jnp.log1p(jnp.exp(-jnp.abs(x)))


def _upper_tri(n):
    r = lax.broadcasted_iota(jnp.int32, (n, n), 0)
    c = lax.broadcasted_iota(jnp.int32, (n, n), 1)
    return (r <= c).astype(BF16)


AUG_ROWS = 16
N_SPLIT = 3


def _bias_select():
    G = N_HEADS // N_KV_FOX
    r = lax.broadcasted_iota(jnp.int32, (N_KV_FOX * AUG_ROWS, N_SPLIT * N_HEADS), 0)
    c = lax.broadcasted_iota(jnp.int32, (N_KV_FOX * AUG_ROWS, N_SPLIT * N_HEADS), 1)
    kv, rr = r // AUG_ROWS, r % AUG_ROWS
    g, part = rr // N_SPLIT, rr % N_SPLIT
    return ((rr < G * N_SPLIT) & (c == part * N_HEADS + kv * G + g)).astype(BF16)


def _fox_proj_kernel(x_ref, g_ref, wq_ref, wkvT_ref, wva_ref, wfT_ref, bf_ref, tri_ref, sel_ref,
                     q_ref, kT_ref, vT_ref, kTa_ref, va_ref, lfT_ref, carry_ref, *, q_scale):
    ti = pl.program_id(1)
    tm = x_ref.shape[1]
    h = _rms_bf16(x_ref[0], g_ref[...])
    q = jnp.dot(h, wq_ref[...], preferred_element_type=F32)
    q_ref[0] = (q * q_scale).astype(BF16)
    kv = lax.dot_general(wkvT_ref[...], h, _NT, preferred_element_type=F32)
    nk = N_KV_FOX * HEAD_DIM
    kT_ref[0] = kv[:nk].reshape(kT_ref.shape[1:])
    vT_ref[0] = kv[nk:].reshape(vT_ref.shape[1:])
    f = lax.dot_general(wfT_ref[...], h, _NT, preferred_element_type=F32)
    lf = _log_sigmoid(f + bf_ref[...])
    lfT_ref[0] = lf

    @pl.when(ti == 0)
    def _():
        carry_ref[...] = jnp.zeros_like(carry_ref)

    hi, mid, lo = _split3(lf)
    tri = tri_ref[...]
    c = (jnp.dot(hi, tri, preferred_element_type=F32) + jnp.dot(mid, tri, preferred_element_type=F32)
         + jnp.dot(lo, tri, preferred_element_type=F32)) + carry_ref[:, 0:1]
    carry_ref[...] = jnp.broadcast_to(c[:, -1:], carry_ref.shape)

    parts = jnp.concatenate(_split3(c * (-LOG2E)), axis=0)
    aug = jnp.dot(sel_ref[...], parts, preferred_element_type=F32).astype(BF16)
    zeros = jnp.zeros((LANES - HEAD_DIM - AUG_ROWS, tm), BF16)
    for kvh in range(N_KV_FOX):
        kTa_ref[0, kvh] = jnp.concatenate(
            [kv[kvh * HEAD_DIM:(kvh + 1) * HEAD_DIM].astype(BF16), aug[kvh * AUG_ROWS:(kvh + 1) * AUG_ROWS], zeros],
            axis=0)
    va = jnp.dot(h, wva_ref[...], preferred_element_type=F32)
    lane = lax.broadcasted_iota(jnp.int32, va.shape, 1)
    va_ref[0] = jnp.where(lane % LANES < HEAD_DIM, va, 1.0).astype(BF16)


def _fox_proj(x, g, wq, wkvT, wva, wfT, bf, tm, q_scale):
    B, S, D = x.shape
    tm = min(tm, S)
    nt = S // tm
    kvshape = (B, N_KV_FOX, HEAD_DIM, S)
    kvblock = pl.BlockSpec((1, N_KV_FOX, HEAD_DIM, tm), lambda b, t: (b, 0, 0, t))
    hblock = pl.BlockSpec((1, N_HEADS, tm), lambda b, t: (b, 0, t))
    const = lambda shape: pl.BlockSpec(shape, lambda b, t: (0,) * len(shape))
    sel = _bias_select()
    return pl.pallas_call(
        functools.partial(_fox_proj_kernel, q_scale=q_scale),
        grid=(B, nt),
        in_specs=[pl.BlockSpec((1, tm, D), lambda b, t: (b, t, 0)), const(g.shape), const(wq.shape),
                  const(wkvT.shape), const(wva.shape), const(wfT.shape), const(bf.shape), const((tm, tm)),
                  const(sel.shape)],
        out_specs=[pl.BlockSpec((1, tm, N_HEADS * HEAD_DIM), lambda b, t: (b, t, 0)),
                   kvblock, kvblock,
                   pl.BlockSpec((1, N_KV_FOX, LANES, tm), lambda b, t: (b, 0, 0, t)),
                   pl.BlockSpec((1, tm, N_KV_FOX * LANES), lambda b, t: (b, t, 0)), hblock],
        out_shape=[jax.ShapeDtypeStruct((B, S, N_HEADS * HEAD_DIM), BF16),
                   jax.ShapeDtypeStruct(kvshape, F32), jax.ShapeDtypeStruct(kvshape, F32),
                   jax.ShapeDtypeStruct((B, N_KV_FOX, LANES, S), BF16),
                   jax.ShapeDtypeStruct((B, S, N_KV_FOX * LANES), BF16),
                   jax.ShapeDtypeStruct((B, N_HEADS, S), F32)],
        scratch_shapes=[pltpu.VMEM((N_HEADS, LANES), F32)],
        compiler_params=_cparams("arbitrary", "arbitrary"),
        name="fox_proj",
    )(x, g, wq, wkvT, wva, wfT, bf, _upper_tri(tm), sel)


def _fox_prompt_attn_kernel(q_ref, kTa_ref, va_ref, o_ref, qa_ref, m_ref, acc_ref, *, tq):
    qi = pl.program_id(2)
    G = N_HEADS // N_KV_FOX
    lane = lax.broadcasted_iota(jnp.int32, (tq, LANES), 1)
    qf = q_ref[0].astype(F32)
    for g in range(G):
        lo = HEAD_DIM + g * N_SPLIT
        ones = jnp.where((lane >= lo) & (lane < lo + N_SPLIT), 1.0, 0.0)
        qg = qf if g == 0 else pltpu.roll(qf, LANES - g * HEAD_DIM, 1)
        qa_ref[g * tq:(g + 1) * tq, :] = jnp.where(lane < HEAD_DIM, qg, ones).astype(BF16)
    m_ref[...] = jnp.full_like(m_ref, NEG_INF)
    acc_ref[...] = jnp.zeros_like(acc_ref)
    reps = tq // LANES

    def chunk(k0, diag):
        kTa = kTa_ref[0, 0, :, pl.ds(k0, tq)]
        va = va_ref[0, pl.ds(k0, tq), :]
        s = jnp.dot(qa_ref[...], kTa, preferred_element_type=F32)
        if diag:
            row = lax.broadcasted_iota(jnp.int32, s.shape, 0) % tq
            col = lax.broadcasted_iota(jnp.int32, s.shape, 1)
            s = jnp.where(col <= row, s, NEG_INF)
        m_prev = m_ref[...]
        m_new = jnp.maximum(m_prev, jnp.max(s, axis=-1, keepdims=True))
        p = jnp.exp2(s - jnp.concatenate([m_new] * reps, axis=1))
        alpha = jnp.exp2(m_prev - m_new)
        acc_ref[...] = alpha * acc_ref[...] + jnp.dot(p.astype(BF16), va, preferred_element_type=F32)
        m_ref[...] = m_new

    def body(i, carry):
        chunk(pl.multiple_of(i * tq, tq), False)
        return carry

    lax.fori_loop(0, qi, body, 0)
    chunk(pl.multiple_of(qi * tq, tq), True)
    acc = acc_ref[...]
    o = acc[:, :HEAD_DIM] / acc[:, HEAD_DIM:]
    o_ref[0] = jnp.concatenate([o[g * tq:(g + 1) * tq] for g in range(G)], axis=1).astype(BF16)


def _fox_prompt_attn(q, kTa, va, tq):
    B, S, _ = q.shape
    tq = min(tq, S)
    G = N_HEADS // N_KV_FOX
    assert G * HEAD_DIM == LANES and HEAD_DIM + G * N_SPLIT <= HEAD_DIM + AUG_ROWS
    return pl.pallas_call(
        functools.partial(_fox_prompt_attn_kernel, tq=tq),
        grid=(B, N_KV_FOX, S // tq),
        in_specs=[pl.BlockSpec((1, tq, LANES), lambda b, h, i: (b, i, h)),
                  pl.BlockSpec((1, 1, LANES, S), lambda b, h, i: (b, h, 0, 0)),
                  pl.BlockSpec((1, S, LANES), lambda b, h, i: (b, 0, h))],
        out_specs=pl.BlockSpec((1, tq, LANES), lambda b, h, i: (b, i, h)),
        out_shape=jax.ShapeDtypeStruct((B, S, N_HEADS * HEAD_DIM), BF16),
        scratch_shapes=[pltpu.VMEM((G * tq, LANES), BF16), pltpu.VMEM((G * tq, LANES), F32),
                        pltpu.VMEM((G * tq, LANES), F32)],
        compiler_params=_cparams("arbitrary", "arbitrary", "arbitrary"),
        name="fox_prompt_attn",
    )(q, kTa, va)


def _fox_decode_kernel(pt_ref, q_ref, kn_ref, vn_ref, lfn_ref, tri_ref, kc_hbm, vc_hbm, lc_hbm,
                       o_ref, kbuf, vbuf, lbuf, sems, qbd_ref, m_ref, l_ref, acc_ref, carry_ref,
                       *, pb, nblk, page_base):
    s_idx = pl.program_id(0)
    j = pl.program_id(1)
    t = s_idx * nblk + j
    n_steps = pl.num_programs(0) * nblk
    slot = lax.rem(t, 2)
    page = LANES
    nh = N_HEADS

    def copies(seq, blk, sl):
        out = []
        for i in range(pb):
            pid = pt_ref[seq, blk * pb + i] + page_base
            out.append(pltpu.make_async_copy(kc_hbm.at[pid], kbuf.at[sl, :, pl.ds(i * page, page)], sems.at[sl, 0]))
            out.append(pltpu.make_async_copy(vc_hbm.at[pid], vbuf.at[sl, :, pl.ds(i * page, page)], sems.at[sl, 1]))
            out.append(pltpu.make_async_copy(lc_hbm.at[pid], lbuf.at[sl, pl.ds(i * nh, nh), :], sems.at[sl, 2]))
        return out

    @pl.when(t == 0)
    def _():
        for cp in copies(s_idx, j, slot):
            cp.start()

    @pl.when(t + 1 < n_steps)
    def _():
        last = j + 1 == nblk
        for cp in copies(jnp.where(last, s_idx + 1, s_idx), jnp.where(last, 0, j + 1), 1 - slot):
            cp.start()

    @pl.when(j == 0)
    def _():
        q16 = q_ref[0].astype(F32)
        qt = jnp.concatenate([q16] * N_KV_FOX, axis=1)
        r = lax.broadcasted_iota(jnp.int32, qt.shape, 0) // (N_HEADS // N_KV_FOX)
        c = lax.broadcasted_iota(jnp.int32, qt.shape, 1) // HEAD_DIM
        qbd_ref[...] = jnp.where(r == c, qt, 0.0).astype(BF16)
        m_ref[...] = jnp.full_like(m_ref, NEG_INF)
        l_ref[...] = jnp.zeros_like(l_ref)
        acc_ref[...] = jnp.zeros_like(acc_ref)
        carry_ref[...] = jnp.zeros_like(carry_ref)

    for cp in copies(s_idx, j, slot):
        cp.wait()

    qbd = qbd_ref[...]
    x = lbuf[slot]
    n = pb * nh
    w3 = jnp.dot(jnp.concatenate(_split3(x), axis=0), tri_ref[...], preferred_element_type=F32)
    w = w3[:n] + w3[n:2 * n] + w3[2 * n:]
    off = carry_ref[:, 0:1]
    cs = []
    for i in range(pb):
        wi = w[i * nh:(i + 1) * nh]
        cs.append(wi + off)
        off = off + wi[:, page - 1:page]
    carry_ref[...] = jnp.broadcast_to(off, carry_ref.shape)
    bias = jnp.concatenate(cs, axis=1)

    s = jnp.dot(qbd, kbuf[slot].astype(BF16), preferred_element_type=F32) - bias
    m_prev = m_ref[...]
    m_new = jnp.maximum(m_prev, jnp.max(s, axis=-1, keepdims=True))
    p = jnp.exp(s - m_new)
    alpha = jnp.exp(m_prev - m_new)
    l_ref[...] = alpha * l_ref[...] + jnp.sum(p, axis=-1, keepdims=True)
    acc_ref[...] = alpha * acc_ref[...] + lax.dot_general(
        p.astype(BF16), vbuf[slot].astype(BF16), _NT, preferred_element_type=F32)
    m_ref[...] = m_new

    @pl.when(j == nblk - 1)
    def _():
        kn = kn_ref[0].astype(BF16).astype(F32)
        s_n = jnp.sum(qbd.astype(F32) * kn, axis=-1, keepdims=True) - (off + lfn_ref[0])
        m_p = m_ref[...]
        m_f = jnp.maximum(m_p, s_n)
        a = jnp.exp(m_p - m_f)
        p_n = jnp.exp(s_n - m_f)
        l_f = a * l_ref[...] + p_n
        acc = (a * acc_ref[...] + p_n * vn_ref[0].astype(BF16).astype(F32)) / l_f
        rr = lax.broadcasted_iota(jnp.int32, (nh, HEAD_DIM), 0) // (N_HEADS // N_KV_FOX)
        o = jnp.zeros((nh, HEAD_DIM), F32)
        for hh in range(N_KV_FOX):
            o = jnp.where(rr == hh, acc[:, hh * HEAD_DIM:(hh + 1) * HEAD_DIM], o)
        o_ref[0] = o.astype(BF16)


def _fox_decode_attn(page_table, q, k_new, v_new, lf_new, kc, vc, lc, page_base, pb):
    N, npages = page_table.shape
    pb = math.gcd(pb, npages)
    nblk = npages // pb
    page = kc.shape[-1]
    assert page == LANES and lc.shape[-1] == LANES
    khd = N_KV_FOX * HEAD_DIM
    grid_spec = pltpu.PrefetchScalarGridSpec(
        num_scalar_prefetch=1,
        grid=(N, nblk),
        in_specs=[pl.BlockSpec((1, N_HEADS, HEAD_DIM), lambda s, j, pt: (s, 0, 0)),
                  pl.BlockSpec((1, 1, khd), lambda s, j, pt: (s, 0, 0)),
                  pl.BlockSpec((1, 1, khd), lambda s, j, pt: (s, 0, 0)),
                  pl.BlockSpec((1, N_HEADS, 1), lambda s, j, pt: (s, 0, 0)),
                  pl.BlockSpec((page, page), lambda s, j, pt: (0, 0)),
                  pl.BlockSpec(memory_space=pl.ANY), pl.BlockSpec(memory_space=pl.ANY),
                  pl.BlockSpec(memory_space=pl.ANY)],
        out_specs=pl.BlockSpec((1, N_HEADS, HEAD_DIM), lambda s, j, pt: (s, 0, 0)),
        scratch_shapes=[pltpu.VMEM((2, khd, pb * page), F32), pltpu.VMEM((2, khd, pb * page), F32),
                        pltpu.VMEM((2, pb * N_HEADS, page), F32), pltpu.SemaphoreType.DMA((2, 3)),
                        pltpu.VMEM((N_HEADS, khd), BF16), pltpu.VMEM((N_HEADS, 1), F32),
                        pltpu.VMEM((N_HEADS, 1), F32), pltpu.VMEM((N_HEADS, khd), F32),
                        pltpu.VMEM((N_HEADS, LANES), F32)],
    )
    return pl.pallas_call(
        functools.partial(_fox_decode_kernel, pb=pb, nblk=nblk, page_base=page_base),
        grid_spec=grid_spec,
        out_shape=jax.ShapeDtypeStruct((N, N_HEADS, HEAD_DIM), BF16),
        compiler_params=_cparams("arbitrary", "arbitrary"),
        name="fox_decode_attn",
    )(page_table, q, k_new, v_new, lf_new, _upper_tri(page), kc, vc, lc)


def _rope_rows(x, cos, sin_signed):
    n = x.shape[1]
    half = HEAD_DIM // 2
    lane = lax.broadcasted_iota(jnp.int32, x.shape, 1)
    rot = jnp.where(lane % HEAD_DIM < half, pltpu.roll(x, n - half, 1), pltpu.roll(x, half, 1))
    reps = n // LANES
    return x * jnp.concatenate([cos] * reps, axis=1) + rot * jnp.concatenate([sin_signed] * reps, axis=1)


def _rope_cols(xT, cosT, sinT):
    half = HEAD_DIM // 2
    out = []
    for h in range(xT.shape[0] // HEAD_DIM):
        x1 = xT[h * HEAD_DIM:h * HEAD_DIM + half]
        x2 = xT[h * HEAD_DIM + half:(h + 1) * HEAD_DIM]
        out += [x1 * cosT - x2 * sinT, x2 * cosT + x1 * sinT]
    return jnp.concatenate(out, axis=0)


def _swa_proj_kernel(x_ref, g_ref, wq_ref, wkvT_ref, wkv_ref, cos_ref, sin_ref, cosT_ref, sinT_ref,
                     q_ref, kT_ref, vT_ref, kTb_ref, vb_ref, kn_ref, vn_ref, *, q_scale):
    h = _rms_bf16(x_ref[0], g_ref[...])
    q = jnp.dot(h, wq_ref[...], preferred_element_type=F32)
    q_ref[0] = (_rope_rows(q, cos_ref[...], sin_ref[...]) * q_scale).astype(BF16)
    kv = lax.dot_general(wkvT_ref[...], h, _NT, preferred_element_type=F32)
    nk = N_KV_SWA * HEAD_DIM
    k = _rope_cols(kv[:nk], cosT_ref[...], sinT_ref[...])
    kT_ref[0] = k
    vT_ref[0] = kv[nk:]
    kTb_ref[0] = k.astype(BF16)
    kvn = jnp.dot(h, wkv_ref[...], preferred_element_type=F32)
    kn_ref[0] = _rope_rows(kvn[:, :nk], cos_ref[...], sin_ref[...])
    vn_ref[0] = kvn[:, nk:]
    vb_ref[0] = kvn[:, nk:].astype(BF16)


def _swa_proj(x, g, wq, wkvT, wkv, pos, tm, q_scale):
    B, S, D = x.shape
    tm = min(tm, S)
    half = HEAD_DIM // 2
    inv = ROPE_THETA ** (-jnp.arange(half, dtype=F32) / half)
    ang = pos.astype(F32)[:, None] * inv[None, :]
    cos, sin = jnp.cos(ang), jnp.sin(ang)
    cos_rows = jnp.concatenate([cos] * (LANES // half), axis=1)
    sin_rows = jnp.concatenate([-sin, sin] * (LANES // HEAD_DIM), axis=1)
    nk = N_KV_SWA * HEAD_DIM
    const = lambda shape: pl.BlockSpec(shape, lambda b, t: (0,) * len(shape))
    tspec = pl.BlockSpec((1, nk, tm), lambda b, t: (b, 0, t))
    rspec = pl.BlockSpec((1, tm, nk), lambda b, t: (b, t, 0))
    assert nk == LANES
    return pl.pallas_call(
        functools.partial(_swa_proj_kernel, q_scale=q_scale),
        grid=(B, S // tm),
        in_specs=[pl.BlockSpec((1, tm, D), lambda b, t: (b, t, 0)), const(g.shape), const(wq.shape),
                  const(wkvT.shape), const(wkv.shape),
                  pl.BlockSpec((tm, LANES), lambda b, t: (t, 0)), pl.BlockSpec((tm, LANES), lambda b, t: (t, 0)),
                  pl.BlockSpec((half, tm), lambda b, t: (0, t)), pl.BlockSpec((half, tm), lambda b, t: (0, t))],
        out_specs=[pl.BlockSpec((1, tm, N_HEADS * HEAD_DIM), lambda b, t: (b, t, 0)),
                   tspec, tspec, tspec, rspec, rspec, rspec],
        out_shape=[jax.ShapeDtypeStruct((B, S, N_HEADS * HEAD_DIM), BF16),
                   jax.ShapeDtypeStruct((B, nk, S), F32), jax.ShapeDtypeStruct((B, nk, S), F32),
                   jax.ShapeDtypeStruct((B, nk, S), BF16), jax.ShapeDtypeStruct((B, S, nk), BF16),
                   jax.ShapeDtypeStruct((B, S, nk), F32), jax.ShapeDtypeStruct((B, S, nk), F32)],
        compiler_params=_cparams("arbitrary", "arbitrary"),
        name="swa_proj",
    )(x, g, wq, wkvT, wkv, cos_rows, sin_rows, cos.T, sin.T)


def _swa_prompt_attn_kernel(sink_ref, q_ref, kT_ref, vb_ref, o_ref, *, tq):
    qi = pl.program_id(1)
    G = N_HEADS // N_KV_SWA
    W = WINDOW
    r = lax.broadcasted_iota(jnp.int32, (G * W, 2 * W), 0) % W
    c = lax.broadcasted_iota(jnp.int32, (G * W, 2 * W), 1)
    ones = jnp.ones((2 * W, LANES), BF16)
    for n in range(tq // W):
        blk = qi * (tq // W) + n
        k0 = pl.multiple_of(jnp.maximum(blk - 1, 0) * W, W)
        delta = (blk * W + r) - (k0 + c)
        mask = jnp.where(delta >= 0, delta, W + 1) <= W
        outs = []
        for kv in range(N_KV_SWA):
            q8 = jnp.concatenate(
                [q_ref[0, n * W:(n + 1) * W, (kv * G + g) * HEAD_DIM:(kv * G + g + 1) * HEAD_DIM]
                 for g in range(G)], axis=0)
            sink = jnp.concatenate(
                [jnp.full((W, LANES), sink_ref[kv * G + g] * LOG2E, F32) for g in range(G)], axis=0)
            kT = kT_ref[0, kv * HEAD_DIM:(kv + 1) * HEAD_DIM, pl.ds(k0, 2 * W)]
            v = vb_ref[0, pl.ds(k0, 2 * W), kv * HEAD_DIM:(kv + 1) * HEAD_DIM]
            s = jnp.where(mask, jnp.dot(q8, kT, preferred_element_type=F32), NEG_INF)
            m = jnp.maximum(jnp.max(s, axis=-1, keepdims=True), sink)
            p = jnp.exp2(s - jnp.concatenate([m] * (2 * W // LANES), axis=1)).astype(BF16)
            den = jnp.dot(p, ones, preferred_element_type=F32) + jnp.exp2(sink - m)
            o8 = jnp.dot(p, v, preferred_element_type=F32) / den[:, :HEAD_DIM]
            outs += [o8[g * W:(g + 1) * W] for g in range(G)]
        o_ref[0, n * W:(n + 1) * W, :] = jnp.concatenate(outs, axis=1).astype(BF16)


def _swa_prompt_attn(sinks, q, kTb, vb, tq):
    B, S, _ = q.shape
    assert S % WINDOW == 0 and S >= 2 * WINDOW
    tq = min(tq, S)
    nk = N_KV_SWA * HEAD_DIM
    grid_spec = pltpu.PrefetchScalarGridSpec(
        num_scalar_prefetch=1,
        grid=(B, S // tq),
        in_specs=[pl.BlockSpec((1, tq, N_HEADS * HEAD_DIM), lambda b, i, sk: (b, i, 0)),
                  pl.BlockSpec((1, nk, S), lambda b, i, sk: (b, 0, 0)),
                  pl.BlockSpec((1, S, nk), lambda b, i, sk: (b, 0, 0))],
        out_specs=pl.BlockSpec((1, tq, N_HEADS * HEAD_DIM), lambda b, i, sk: (b, i, 0)),
    )
    return pl.pallas_call(
        functools.partial(_swa_prompt_attn_kernel, tq=tq),
        grid_spec=grid_spec,
        out_shape=jax.ShapeDtypeStruct((B, S, N_HEADS * HEAD_DIM), BF16),
        compiler_params=_cparams("arbitrary", "arbitrary"),
        name="swa_prompt_attn",
    )(sinks, q, kTb, vb)


def _swa_decode_kernel(q_ref, kc_ref, vc_ref, kn_ref, vn_ref, knT_ref, vnT_ref, sink_ref,
                       o_ref, ko_ref, vo_ref):
    G = N_HEADS // N_KV_SWA
    nk = N_KV_SWA * HEAD_DIM
    bs, _, wb = kc_ref.shape
    sink = sink_ref[...]
    rh = lax.broadcasted_iota(jnp.int32, (N_HEADS, HEAD_DIM), 0) // G
    lane = lax.broadcasted_iota(jnp.int32, (nk, wb), 1)
    seq_lane = lax.broadcasted_iota(jnp.int32, knT_ref.shape, 1)
    for i in range(bs):
        s_idx = pl.program_id(0) * bs + i
        q16 = q_ref[i].astype(F32)
        qt = jnp.concatenate([q16] * N_KV_SWA, axis=1)
        rr = lax.broadcasted_iota(jnp.int32, qt.shape, 0) // G
        cc = lax.broadcasted_iota(jnp.int32, qt.shape, 1) // HEAD_DIM
        qbd = jnp.where(rr == cc, qt, 0.0)
        kT = kc_ref[i]
        vT = vc_ref[i]
        s_p = jnp.dot(qbd.astype(BF16), kT.astype(BF16), preferred_element_type=F32)
        s_n = jnp.sum(qbd.astype(BF16).astype(F32) * kn_ref[i].astype(BF16).astype(F32), axis=-1, keepdims=True)
        m = jnp.maximum(jnp.maximum(jnp.max(s_p, axis=-1, keepdims=True), s_n), sink)
        p_p = jnp.exp(s_p - m)
        p_n = jnp.exp(s_n - m)
        den = jnp.sum(p_p, axis=-1, keepdims=True) + p_n + jnp.exp(sink - m)
        acc = lax.dot_general((p_p / den).astype(BF16), vT.astype(BF16), _NT, preferred_element_type=F32)
        acc = acc + (p_n / den).astype(BF16).astype(F32) * vn_ref[i].astype(BF16).astype(F32)
        o = jnp.zeros((N_HEADS, HEAD_DIM), F32)
        for kv in range(N_KV_SWA):
            o = jnp.where(rh == kv, acc[:, kv * HEAD_DIM:(kv + 1) * HEAD_DIM], o)
        o_ref[i] = o.astype(BF16)
        k_col = jnp.sum(jnp.where(seq_lane == s_idx, knT_ref[...], 0.0), axis=-1, keepdims=True)
        v_col = jnp.sum(jnp.where(seq_lane == s_idx, vnT_ref[...], 0.0), axis=-1, keepdims=True)
        ko_ref[i] = jnp.where(lane == wb - 1, k_col, pltpu.roll(kT, wb - 1, 1))
        vo_ref[i] = jnp.where(lane == wb - 1, v_col, pltpu.roll(vT, wb - 1, 1))


def _swa_decode_attn(q, kc, vc, k_new, v_new, k_newT, v_newT, sinks, bs):
    N, nk, wb = kc.shape
    assert wb == LANES and wb <= WINDOW
    bs = math.gcd(bs, N)
    seq3 = lambda shape: pl.BlockSpec((bs,) + shape, lambda s: (s, 0, 0))
    const = lambda shape: pl.BlockSpec(shape, lambda s: (0,) * len(shape))
    return pl.pallas_call(
        _swa_decode_kernel,
        grid=(N // bs,),
        in_specs=[seq3((N_HEADS, HEAD_DIM)), seq3((nk, wb)), seq3((nk, wb)), seq3((1, nk)), seq3((1, nk)),
                  const(k_newT.shape), const(v_newT.shape), const(sinks.shape)],
        out_specs=[seq3((N_HEADS, HEAD_DIM)), seq3((nk, wb)), seq3((nk, wb))],
        out_shape=[jax.ShapeDtypeStruct((N, N_HEADS, HEAD_DIM), BF16),
                   jax.ShapeDtypeStruct((N, nk, wb), F32), jax.ShapeDtypeStruct((N, nk, wb), F32)],
        compiler_params=_cparams("arbitrary"),
        name="swa_decode_attn",
    )(q, kc, vc, k_new, v_new, k_newT, v_newT, sinks)


def _silu(x):
    return x * (1.0 / (1.0 + jnp.exp(-x)))


def _post_prompt_kernel(x_ref, o_ref, wo_ref, g_ref, win_ref, cw_ref, cb_ref, wout_ref, gf_ref,
                        y_ref, ulast_ref, halo_ref, ubuf_ref, *, n_chunks, final_norm):
    ti = pl.program_id(1)
    tm = x_ref.shape[1]
    F = wout_ref.shape[0]
    fc = F // n_chunks
    pad = ubuf_ref.shape[0] - tm

    @pl.when(ti == 0)
    def _():
        halo_ref[...] = jnp.zeros_like(halo_ref)

    xn = x_ref[0] + jnp.dot(o_ref[0], wo_ref[...], preferred_element_type=F32)
    h = _rms_bf16(xn, g_ref[...])
    y = jnp.zeros_like(xn)
    for ci in range(n_chunks):
        halves = []
        for c0 in (ci * fc, F + ci * fc):
            u = jnp.dot(h, win_ref[:, c0:c0 + fc], preferred_element_type=F32)
            ubuf_ref[0:pad, :] = halo_ref[:, c0:c0 + fc]
            ubuf_ref[pad:pad + tm, :] = u
            halo_ref[:, c0:c0 + fc] = u[tm - pad:]
            ulast_ref[0, :, c0:c0 + fc] = u[tm - pad:]
            conv = (cb_ref[:, c0:c0 + fc] + cw_ref[0:1, c0:c0 + fc] * ubuf_ref[pad - 2:pad - 2 + tm, :]
                    + cw_ref[1:2, c0:c0 + fc] * ubuf_ref[pad - 1:pad - 1 + tm, :]
                    + cw_ref[2:3, c0:c0 + fc] * u)
            halves.append(conv)
        a = (_silu(halves[0]) * halves[1]).astype(BF16)
        y = y + jnp.dot(a, wout_ref[ci * fc:(ci + 1) * fc, :], preferred_element_type=F32)
    out = xn + y
    if final_norm:
        out = out * lax.rsqrt(jnp.mean(out * out, axis=-1, keepdims=True) + RMS_EPS) * gf_ref[...]
    y_ref[0] = out


def _post_prompt(x, o, wo, g, win, cw, cb, wout, gf, final_norm, tm, n_chunks):
    B, S, D = x.shape
    tm = min(tm, S)
    F2 = win.shape[1]
    pad = 8
    const = lambda shape: pl.BlockSpec(shape, lambda b, t: (0,) * len(shape))
    row = pl.BlockSpec((1, tm, D), lambda b, t: (b, t, 0))
    return pl.pallas_call(
        functools.partial(_post_prompt_kernel, n_chunks=n_chunks, final_norm=final_norm),
        grid=(B, S // tm),
        in_specs=[row, row, const(wo.shape), const(g.shape), const(win.shape), const(cw.shape),
                  const(cb.shape), const(wout.shape), const(gf.shape)],
        out_specs=[row, pl.BlockSpec((1, pad, F2), lambda b, t: (b, 0, 0))],
        out_shape=[jax.ShapeDtypeStruct((B, S, D), F32), jax.ShapeDtypeStruct((B, pad, F2), F32)],
        scratch_shapes=[pltpu.VMEM((pad, F2), F32), pltpu.VMEM((tm + pad, F2 // (2 * n_chunks)), F32)],
        compiler_params=_cparams("arbitrary", "arbitrary"),
        name="post_prompt",
    )(x, o, wo, g, win, cw, cb, wout, gf)


def _post_sample_kernel(x_ref, o_ref, wo_ref, g_ref, win_ref, cw_ref, cb_ref, wout_ref, gf_ref,
                        s0_ref, s1_ref, y_ref, u_ref, *, n_chunks, final_norm):
    F = wout_ref.shape[0]
    fc = F // n_chunks
    xn = x_ref[...] + jnp.dot(o_ref[...], wo_ref[...], preferred_element_type=F32)
    h = _rms_bf16(xn, g_ref[...])
    y = jnp.zeros_like(xn)
    for ci in range(n_chunks):
        halves = []
        for c0 in (ci * fc, F + ci * fc):
            u = jnp.dot(h, win_ref[:, c0:c0 + fc], preferred_element_type=F32)
            u_ref[:, c0:c0 + fc] = u
            halves.append(cb_ref[:, c0:c0 + fc] + cw_ref[0:1, c0:c0 + fc] * s0_ref[:, c0:c0 + fc]
                          + cw_ref[1:2, c0:c0 + fc] * s1_ref[:, c0:c0 + fc] + cw_ref[2:3, c0:c0 + fc] * u)
        a = (_silu(halves[0]) * halves[1]).astype(BF16)
        y = y + jnp.dot(a, wout_ref[ci * fc:(ci + 1) * fc, :], preferred_element_type=F32)
    out = xn + y
    if final_norm:
        out = out * lax.rsqrt(jnp.mean(out * out, axis=-1, keepdims=True) + RMS_EPS) * gf_ref[...]
    y_ref[...] = out


def _post_sample(x, o, wo, g, win, cw, cb, wout, gf, s0, s1, final_norm, n_chunks):
    N, D = x.shape
    F2 = win.shape[1]
    return pl.pallas_call(
        functools.partial(_post_sample_kernel, n_chunks=n_chunks, final_norm=final_norm),
        out_shape=[jax.ShapeDtypeStruct((N, D), F32), jax.ShapeDtypeStruct((N, F2), F32)],
        compiler_params=_cparams(),
        name="post_sample",
    )(x, o, wo, g, win, cw, cb, wout, gf, s0, s1)


PROJ_TILE = 512
POST_TILE = 512
FFN_CHUNKS = 2
FOX_TQ = 512
SWA_TQ = 512
DECODE_PAGES_PER_BLOCK = 16
SWA_DECODE_SEQS = 8


def kernel(x_prompt, x_sample, cache_fox_k, cache_fox_v, cache_fox_logf, cache_swa_k, cache_swa_v, state_ffn_conv, page_table, norm_mixer, norm_ffn, norm_final, fox_w_qkv, fox_b_f, fox_w_o, swa_w_qkv, swa_sinks, swa_w_o, ffn_w_in, ffn_conv_w, ffn_conv_b, ffn_w_out):
    B, S, D = x_prompt.shape
    N = x_sample.shape[0]
    assert x_sample.shape[1] == 1
    depth = norm_mixer.shape[0]
    n_phys, page = cache_fox_k.shape[1], cache_fox_k.shape[2]
    past_len = page_table.shape[1] * page
    nq = N_HEADS * HEAD_DIM
    nkf = N_KV_FOX * HEAD_DIM
    nks = N_KV_SWA * HEAD_DIM
    F = ffn_w_out.shape[1]

    xp = x_prompt
    xs = x_sample.reshape(1, N, D)
    gfin = norm_final.reshape(1, D)

    kc = jnp.transpose(cache_fox_k, (0, 1, 3, 4, 2)).reshape(-1, nkf, page)
    vc = jnp.transpose(cache_fox_v, (0, 1, 3, 4, 2)).reshape(-1, nkf, page)
    lc = jnp.transpose(cache_fox_logf, (0, 1, 3, 2)).reshape(-1, N_HEADS, page)

    fkp, fvp, flp, fks, fvs, fls = [], [], [], [], [], []
    skp, svp, sks, svs = [], [], [], []
    cps, css = [], []
    for i in range(depth):
        j = i // 2
        g_mix = norm_mixer[i].reshape(1, D)
        if i % 2 == 0:
            w = fox_w_qkv[j]
            wq = w[:, :nq].astype(BF16)
            wkvT = w[:, nq:nq + 2 * nkf].T.astype(BF16)
            wva = jnp.pad(w[:, nq + nkf:nq + 2 * nkf].reshape(D, N_KV_FOX, HEAD_DIM),
                          ((0, 0), (0, 0), (0, LANES - HEAD_DIM))).reshape(D, N_KV_FOX * LANES).astype(BF16)
            wfT = w[:, nq + 2 * nkf:].T.astype(BF16)
            bf = fox_b_f[j].reshape(N_HEADS, 1)
            q, kT, vT, kTa, va, lfT = _fox_proj(xp, g_mix, wq, wkvT, wva, wfT, bf, PROJ_TILE, SCALE * LOG2E)
            op = _fox_prompt_attn(q, kTa, va, FOX_TQ)
            fkp.append(jnp.transpose(kT, (0, 3, 1, 2)))
            fvp.append(jnp.transpose(vT, (0, 3, 1, 2)))
            flp.append(jnp.transpose(lfT, (0, 2, 1)))

            qs, kTs, vTs, _, _, lfTs = _fox_proj(xs, g_mix, wq, wkvT, wva, wfT, bf, PROJ_TILE, SCALE)
            k_new = jnp.transpose(kTs[0], (2, 0, 1)).reshape(N, 1, nkf)
            v_new = jnp.transpose(vTs[0], (2, 0, 1)).reshape(N, 1, nkf)
            lf_new = jnp.transpose(lfTs[0], (1, 0)).reshape(N, N_HEADS, 1)
            os_ = _fox_decode_attn(page_table, qs.reshape(N, N_HEADS, HEAD_DIM), k_new, v_new, lf_new,
                                   kc, vc, lc, j * n_phys, DECODE_PAGES_PER_BLOCK)
            os_ = os_.reshape(N, nq)
            fks.append(jnp.transpose(kTs[0], (2, 0, 1)).reshape(N, 1, N_KV_FOX, HEAD_DIM))
            fvs.append(jnp.transpose(vTs[0], (2, 0, 1)).reshape(N, 1, N_KV_FOX, HEAD_DIM))
            fls.append(jnp.transpose(lfTs[0], (1, 0)).reshape(N, 1, N_HEADS))
            wo = fox_w_o[j].astype(BF16)
        else:
            w = swa_w_qkv[j]
            wq = w[:, :nq].astype(BF16)
            wkv = w[:, nq:].astype(BF16)
            wkvT = w[:, nq:].T.astype(BF16)
            sinks = swa_sinks[j]
            q, kT, vT, kTb, vb, _, _ = _swa_proj(xp, g_mix, wq, wkvT, wkv, jnp.arange(S), PROJ_TILE,
                                                 SCALE * LOG2E)
            op = _swa_prompt_attn(sinks, q, kTb, vb, SWA_TQ)
            keep = min(WINDOW, S)
            skp.append(jnp.transpose(kT[:, :, S - keep:].reshape(B, N_KV_SWA, HEAD_DIM, keep), (0, 3, 1, 2)))
            svp.append(jnp.transpose(vT[:, :, S - keep:].reshape(B, N_KV_SWA, HEAD_DIM, keep), (0, 3, 1, 2)))

            pos_s = jnp.full((N,), past_len, jnp.int32)
            qs, kTs, vTs, _, _, kns, vns = _swa_proj(xs, g_mix, wq, wkvT, wkv, pos_s, PROJ_TILE, SCALE)
            wb = cache_swa_k.shape[2]
            kcs = jnp.transpose(cache_swa_k[j], (0, 2, 3, 1)).reshape(N, nks, wb)
            vcs = jnp.transpose(cache_swa_v[j], (0, 2, 3, 1)).reshape(N, nks, wb)
            os_, ko, vo = _swa_decode_attn(qs.reshape(N, N_HEADS, HEAD_DIM), kcs, vcs,
                                           kns.reshape(N, 1, nks), vns.reshape(N, 1, nks),
                                           kTs[0], vTs[0], sinks.reshape(N_HEADS, 1), SWA_DECODE_SEQS)
            os_ = os_.reshape(N, nq)
            sks.append(jnp.transpose(ko.reshape(N, N_KV_SWA, HEAD_DIM, wb), (0, 3, 1, 2)))
            svs.append(jnp.transpose(vo.reshape(N, N_KV_SWA, HEAD_DIM, wb), (0, 3, 1, 2)))
            wo = swa_w_o[j].astype(BF16)

        g_ffn = norm_ffn[i].reshape(1, D)
        win = ffn_w_in[i].astype(BF16)
        wout = ffn_w_out[i].astype(BF16)
        cw = ffn_conv_w[i]
        cb = ffn_conv_b[i].reshape(1, 2 * F)
        last = i == depth - 1
        xp, ulast = _post_prompt(xp, op, wo, g_ffn, win, cw, cb, wout, gfin, last, POST_TILE, FFN_CHUNKS)
        cps.append(ulast[:, -(CONV_W - 1):])
        st = state_ffn_conv[i]
        xs2, us = _post_sample(xs[0], os_, wo, g_ffn, win, cw, cb, wout, gfin, st[:, 0], st[:, 1],
                               last, FFN_CHUNKS)
        xs = xs2.reshape(1, N, D)
        css.append(jnp.stack([st[:, 1], us], axis=1))

    return (xp, xs.reshape(N, 1, D),
            jnp.stack(fkp), jnp.stack(fvp), jnp.stack(flp),
            jnp.stack(fks), jnp.stack(fvs), jnp.stack(fls),
            jnp.stack(skp), jnp.stack(svp), jnp.stack(sks), jnp.stack(svs),
            jnp.stack(cps), jnp.stack(css))
```

```python
import functools
import math

import jax
import jax.numpy as jnp
from jax import lax
from jax.experimental import pallas as pl
from jax.experimental.pallas import tpu as pltpu

HEAD_DIM = 64
N_HEADS = 16
N_KV_FOX = 8
N_KV_SWA = 2
WINDOW = 128
ROPE_THETA = 10000.0
RMS_EPS = 1e-6
CONV_W = 3
SCALE = HEAD_DIM ** -0.5
LOG2E = math.log2(math.e)

F32 = jnp.float32
BF16 = jnp.bfloat16

V7X_VMEM_LIMIT_BYTES = 56 * 1024 * 1024
LANES = 128
NEG_INF = float("-inf")

_NT = (((1,), (1,)), ((), ()))


def _cparams(*sem):
    return pltpu.CompilerParams(dimension_semantics=sem, vmem_limit_bytes=V7X_VMEM_LIMIT_BYTES)


def _rms_bf16(x, g):
    y = x * lax.rsqrt(jnp.mean(x * x, axis=-1, keepdims=True) + RMS_EPS)
    return (y * g).astype(BF16)


def _split3(x):
    hi = x.astype(BF16)
    r1 = x - hi.astype(F32)
    mid = r1.astype(BF16)
    lo = (r1 - mid.astype(F32)).astype(BF16)
    return hi, mid, lo


def _log_sigmoid(x):
    return jnp.minimum(x, 0.0) - jnp.log1p(jnp.exp(-jnp.abs(x)))


def _upper_tri(n):
    r = lax.broadcasted_iota(jnp.int32, (n, n), 0)
    c = lax.broadcasted_iota(jnp.int32, (n, n), 1)
    return (r <= c).astype(BF16)


AUG_ROWS = 16
N_SPLIT = 3


def _bias_select():
    G = N_HEADS // N_KV_FOX
    r = lax.broadcasted_iota(jnp.int32, (N_KV_FOX * AUG_ROWS, N_SPLIT * N_HEADS), 0)
    c = lax.broadcasted_iota(jnp.int32, (N_KV_FOX * AUG_ROWS, N_SPLIT * N_HEADS), 1)
    kv, rr = r // AUG_ROWS, r % AUG_ROWS
    g, part = rr // N_SPLIT, rr % N_SPLIT
    return ((rr < G * N_SPLIT) & (c == part * N_HEADS + kv * G + g)).astype(BF16)


def _fox_proj_kernel(x_ref, g_ref, wq_ref, wkvT_ref, wva_ref, wfT_ref, bf_ref, tri_ref, sel_ref,
                     q_ref, kT_ref, vT_ref, kTa_ref, va_ref, lfT_ref, carry_ref, *, q_scale):
    ti = pl.program_id(1)
    tm = x_ref.shape[1]
    h = _rms_bf16(x_ref[0], g_ref[...])
    q = jnp.dot(h, wq_ref[...], preferred_element_type=F32)
    q_ref[0] = (q * q_scale).astype(BF16)
    kv = lax.dot_general(wkvT_ref[...], h, _NT, preferred_element_type=F32)
    nk = N_KV_FOX * HEAD_DIM
    kT_ref[0] = kv[:nk].reshape(kT_ref.shape[1:])
    vT_ref[0] = kv[nk:].reshape(vT_ref.shape[1:])
    f = lax.dot_general(wfT_ref[...], h, _NT, preferred_element_type=F32)
    lf = _log_sigmoid(f + bf_ref[...])
    lfT_ref[0] = lf

    @pl.when(ti == 0)
    def _():
        carry_ref[...] = jnp.zeros_like(carry_ref)

    hi, mid, lo = _split3(lf)
    tri = tri_ref[...]
    c = (jnp.dot(hi, tri, preferred_element_type=F32) + jnp.dot(mid, tri, preferred_element_type=F32)
         + jnp.dot(lo, tri, preferred_element_type=F32)) + carry_ref[:, 0:1]
    carry_ref[...] = jnp.broadcast_to(c[:, -1:], carry_ref.shape)

    parts = jnp.concatenate(_split3(c * (-LOG2E)), axis=0)
    aug = jnp.dot(sel_ref[...], parts, preferred_element_type=F32).astype(BF16)
    zeros = jnp.zeros((LANES - HEAD_DIM - AUG_ROWS, tm), BF16)
    for kvh in range(N_KV_FOX):
        kTa_ref[0, kvh] = jnp.concatenate(
            [kv[kvh * HEAD_DIM:(kvh + 1) * HEAD_DIM].astype(BF16), aug[kvh * AUG_ROWS:(kvh + 1) * AUG_ROWS], zeros],
            axis=0)
    va = jnp.dot(h, wva_ref[...], preferred_element_type=F32)
    lane = lax.broadcasted_iota(jnp.int32, va.shape, 1)
    va_ref[0] = jnp.where(lane % LANES < HEAD_DIM, va, 1.0).astype(BF16)


def _fox_proj(x, g, wq, wkvT, wva, wfT, bf, tm, q_scale):
    B, S, D = x.shape
    tm = min(tm, S)
    nt = S // tm
    kvshape = (B, N_KV_FOX, HEAD_DIM, S)
    kvblock = pl.BlockSpec((1, N_KV_FOX, HEAD_DIM, tm), lambda b, t: (b, 0, 0, t))
    hblock = pl.BlockSpec((1, N_HEADS, tm), lambda b, t: (b, 0, t))
    const = lambda shape: pl.BlockSpec(shape, lambda b, t: (0,) * len(shape))
    sel = _bias_select()
    return pl.pallas_call(
        functools.partial(_fox_proj_kernel, q_scale=q_scale),
        grid=(B, nt),
        in_specs=[pl.BlockSpec((1, tm, D), lambda b, t: (b, t, 0)), const(g.shape), const(wq.shape),
                  const(wkvT.shape), const(wva.shape), const(wfT.shape), const(bf.shape), const((tm, tm)),
                  const(sel.shape)],
        out_specs=[pl.BlockSpec((1, tm, N_HEADS * HEAD_DIM), lambda b, t: (b, t, 0)),
                   kvblock, kvblock,
                   pl.BlockSpec((1, N_KV_FOX, LANES, tm), lambda b, t: (b, 0, 0, t)),
                   pl.BlockSpec((1, tm, N_KV_FOX * LANES), lambda b, t: (b, t, 0)), hblock],
        out_shape=[jax.ShapeDtypeStruct((B, S, N_HEADS * HEAD_DIM), BF16),
                   jax.ShapeDtypeStruct(kvshape, F32), jax.ShapeDtypeStruct(kvshape, F32),
                   jax.ShapeDtypeStruct((B, N_KV_FOX, LANES, S), BF16),
                   jax.ShapeDtypeStruct((B, S, N_KV_FOX * LANES), BF16),
                   jax.ShapeDtypeStruct((B, N_HEADS, S), F32)],
        scratch_shapes=[pltpu.VMEM((N_HEADS, LANES), F32)],
        compiler_params=_cparams("arbitrary", "arbitrary"),
        name="fox_proj",
    )(x, g, wq, wkvT, wva, wfT, bf, _upper_tri(tm), sel)


def _fox_prompt_attn_kernel(q_ref, kTa_ref, va_ref, o_ref, qa_ref, m_ref, acc_ref, *, tq, span):
    qi = pl.program_id(2)
    G = N_HEADS // N_KV_FOX
    lane = lax.broadcasted_iota(jnp.int32, (tq, LANES), 1)
    qf = q_ref[0].astype(F32)
    for g in range(G):
        lo = HEAD_DIM + g * N_SPLIT
        ones = jnp.where((lane >= lo) & (lane < lo + N_SPLIT), 1.0, 0.0)
        qg = qf if g == 0 else pltpu.roll(qf, LANES - g * HEAD_DIM, 1)
        qa_ref[g * tq:(g + 1) * tq, :] = jnp.where(lane < HEAD_DIM, qg, ones).astype(BF16)
    m_ref[...] = jnp.full_like(m_ref, NEG_INF)
    acc_ref[...] = jnp.zeros_like(acc_ref)

    def chunk(blk0, nblk, diag):
        tk = nblk * tq
        k0 = pl.multiple_of(blk0 * tq, tk)
        kTa = kTa_ref[0, 0, :, pl.ds(k0, tk)]
        va = va_ref[0, pl.ds(k0, tk), :]
        s = jnp.dot(qa_ref[...], kTa, preferred_element_type=F32)
        if diag:
            row = lax.broadcasted_iota(jnp.int32, s.shape, 0) % tq
            col = lax.broadcasted_iota(jnp.int32, s.shape, 1)
            s = jnp.where(col <= row, s, NEG_INF)
        m_prev = m_ref[...]
        m_new = jnp.maximum(m_prev, jnp.max(s, axis=-1, keepdims=True))
        p = jnp.exp2(s - jnp.concatenate([m_new] * (tk // LANES), axis=1))
        alpha = jnp.exp2(m_prev - m_new)
        acc_ref[...] = alpha * acc_ref[...] + jnp.dot(p.astype(BF16), va, preferred_element_type=F32)
        m_ref[...] = m_new

    def body(i, carry):
        chunk(i * span, span, False)
        return carry

    lax.fori_loop(0, qi // span, body, 0)
    size = span // 2
    while size >= 1:
        @pl.when(qi % (2 * size) >= size)
        def _(size=size):
            chunk(qi // (2 * size) * (2 * size), size, False)
        size //= 2
    chunk(qi, 1, True)
    acc = acc_ref[...]
    o = acc[:, :HEAD_DIM] / acc[:, HEAD_DIM:]
    o_ref[0] = jnp.concatenate([o[g * tq:(g + 1) * tq] for g in range(G)], axis=1).astype(BF16)


def _fox_prompt_attn(q, kTa, va, tq, span):
    B, S, _ = q.shape
    tq = min(tq, S)
    G = N_HEADS // N_KV_FOX
    assert G * HEAD_DIM == LANES and HEAD_DIM + G * N_SPLIT <= HEAD_DIM + AUG_ROWS
    return pl.pallas_call(
        functools.partial(_fox_prompt_attn_kernel, tq=tq, span=span),
        grid=(B, N_KV_FOX, S // tq),
        in_specs=[pl.BlockSpec((1, tq, LANES), lambda b, h, i: (b, i, h)),
                  pl.BlockSpec((1, 1, LANES, S), lambda b, h, i: (b, h, 0, 0)),
                  pl.BlockSpec((1, S, LANES), lambda b, h, i: (b, 0, h))],
        out_specs=pl.BlockSpec((1, tq, LANES), lambda b, h, i: (b, i, h)),
        out_shape=jax.ShapeDtypeStruct((B, S, N_HEADS * HEAD_DIM), BF16),
        scratch_shapes=[pltpu.VMEM((G * tq, LANES), BF16), pltpu.VMEM((G * tq, LANES), F32),
                        pltpu.VMEM((G * tq, LANES), F32)],
        compiler_params=_cparams("arbitrary", "arbitrary", "arbitrary"),
        name="fox_prompt_attn",
    )(q, kTa, va)


def _fox_decode_kernel(pt_ref, q_ref, kn_ref, vn_ref, lfn_ref, tri_ref, kc_hbm, vc_hbm, lc_hbm,
                       o_ref, kbuf, vbuf, lbuf, sems, qbd_ref, m_ref, l_ref, acc_ref, carry_ref,
                       *, pb, nblk, page_base):
    s_idx = pl.program_id(0)
    j = pl.program_id(1)
    t = s_idx * nblk + j
    n_steps = pl.num_programs(0) * nblk
    slot = lax.rem(t, 2)
    page = LANES
    nh = N_HEADS

    def copies(seq, blk, sl):
        out = []
        for i in range(pb):
            pid = pt_ref[seq, blk * pb + i] + page_base
            out.append(pltpu.make_async_copy(kc_hbm.at[pid], kbuf.at[sl, :, pl.ds(i * page, page)], sems.at[sl, 0]))
            out.append(pltpu.make_async_copy(vc_hbm.at[pid], vbuf.at[sl, :, pl.ds(i * page, page)], sems.at[sl, 1]))
            out.append(pltpu.make_async_copy(lc_hbm.at[pid], lbuf.at[sl, pl.ds(i * nh, nh), :], sems.at[sl, 2]))
        return out

    @pl.when(t == 0)
    def _():
        for cp in copies(s_idx, j, slot):
            cp.start()

    @pl.when(t + 1 < n_steps)
    def _():
        last = j + 1 == nblk
        for cp in copies(jnp.where(last, s_idx + 1, s_idx), jnp.where(last, 0, j + 1), 1 - slot):
            cp.start()

    @pl.when(j == 0)
    def _():
        q16 = q_ref[0].astype(F32)
        qt = jnp.concatenate([q16] * N_KV_FOX, axis=1)
        r = lax.broadcasted_iota(jnp.int32, qt.shape, 0) // (N_HEADS // N_KV_FOX)
        c = lax.broadcasted_iota(jnp.int32, qt.shape, 1) // HEAD_DIM
        qbd_ref[...] = jnp.where(r == c, qt, 0.0).astype(BF16)
        m_ref[...] = jnp.full_like(m_ref, NEG_INF)
        l_ref[...] = jnp.zeros_like(l_ref)
        acc_ref[...] = jnp.zeros_like(acc_ref)
        carry_ref[...] = jnp.zeros_like(carry_ref)

    for cp in copies(s_idx, j, slot):
        cp.wait()

    qbd = qbd_ref[...]
    x = lbuf[slot]
    n = pb * nh
    w3 = jnp.dot(jnp.concatenate(_split3(x), axis=0), tri_ref[...], preferred_element_type=F32)
    w = w3[:n] + w3[n:2 * n] + w3[2 * n:]
    off = carry_ref[:, 0:1]
    cs = []
    for i in range(pb):
        wi = w[i * nh:(i + 1) * nh]
        cs.append(wi + off)
        off = off + wi[:, page - 1:page]
    carry_ref[...] = jnp.broadcast_to(off, carry_ref.shape)
    bias = jnp.concatenate(cs, axis=1)

    s = jnp.dot(qbd, kbuf[slot].astype(BF16), preferred_element_type=F32) - bias
    m_prev = m_ref[...]
    m_new = jnp.maximum(m_prev, jnp.max(s, axis=-1, keepdims=True))
    p = jnp.exp(s - m_new)
    alpha = jnp.exp(m_prev - m_new)
    l_ref[...] = alpha * l_ref[...] + jnp.sum(p, axis=-1, keepdims=True)
    acc_ref[...] = alpha * acc_ref[...] + lax.dot_general(
        p.astype(BF16), vbuf[slot].astype(BF16), _NT, preferred_element_type=F32)
    m_ref[...] = m_new

    @pl.when(j == nblk - 1)
    def _():
        kn = kn_ref[0].astype(BF16).astype(F32)
        s_n = jnp.sum(qbd.astype(F32) * kn, axis=-1, keepdims=True) - (off + lfn_ref[0])
        m_p = m_ref[...]
        m_f = jnp.maximum(m_p, s_n)
        a = jnp.exp(m_p - m_f)
        p_n = jnp.exp(s_n - m_f)
        l_f = a * l_ref[...] + p_n
        acc = (a * acc_ref[...] + p_n * vn_ref[0].astype(BF16).astype(F32)) / l_f
        rr = lax.broadcasted_iota(jnp.int32, (nh, HEAD_DIM), 0) // (N_HEADS // N_KV_FOX)
        o = jnp.zeros((nh, HEAD_DIM), F32)
        for hh in range(N_KV_FOX):
            o = jnp.where(rr == hh, acc[:, hh * HEAD_DIM:(hh + 1) * HEAD_DIM], o)
        o_ref[0] = o.astype(BF16)


def _fox_decode_attn(page_table, q, k_new, v_new, lf_new, kc, vc, lc, page_base, pb):
    N, npages = page_table.shape
    pb = math.gcd(pb, npages)
    nblk = npages // pb
    page = kc.shape[-1]
    assert page == LANES and lc.shape[-1] == LANES
    khd = N_KV_FOX * HEAD_DIM
    grid_spec = pltpu.PrefetchScalarGridSpec(
        num_scalar_prefetch=1,
        grid=(N, nblk),
        in_specs=[pl.BlockSpec((1, N_HEADS, HEAD_DIM), lambda s, j, pt: (s, 0, 0)),
                  pl.BlockSpec((1, 1, khd), lambda s, j, pt: (s, 0, 0)),
                  pl.BlockSpec((1, 1, khd), lambda s, j, pt: (s, 0, 0)),
                  pl.BlockSpec((1, N_HEADS, 1), lambda s, j, pt: (s, 0, 0)),
                  pl.BlockSpec((page, page), lambda s, j, pt: (0, 0)),
                  pl.BlockSpec(memory_space=pl.ANY), pl.BlockSpec(memory_space=pl.ANY),
                  pl.BlockSpec(memory_space=pl.ANY)],
        out_specs=pl.BlockSpec((1, N_HEADS, HEAD_DIM), lambda s, j, pt: (s, 0, 0)),
        scratch_shapes=[pltpu.VMEM((2, khd, pb * page), F32), pltpu.VMEM((2, khd, pb * page), F32),
                        pltpu.VMEM((2, pb * N_HEADS, page), F32), pltpu.SemaphoreType.DMA((2, 3)),
                        pltpu.VMEM((N_HEADS, khd), BF16), pltpu.VMEM((N_HEADS, 1), F32),
                        pltpu.VMEM((N_HEADS, 1), F32), pltpu.VMEM((N_HEADS, khd), F32),
                        pltpu.VMEM((N_HEADS, LANES), F32)],
    )
    return pl.pallas_call(
        functools.partial(_fox_decode_kernel, pb=pb, nblk=nblk, page_base=page_base),
        grid_spec=grid_spec,
        out_shape=jax.ShapeDtypeStruct((N, N_HEADS, HEAD_DIM), BF16),
        compiler_params=_cparams("arbitrary", "arbitrary"),
        name="fox_decode_attn",
    )(page_table, q, k_new, v_new, lf_new, _upper_tri(page), kc, vc, lc)


def _rope_rows(x, cos, sin_signed):
    n = x.shape[1]
    half = HEAD_DIM // 2
    lane = lax.broadcasted_iota(jnp.int32, x.shape, 1)
    rot = jnp.where(lane % HEAD_DIM < half, pltpu.roll(x, n - half, 1), pltpu.roll(x, half, 1))
    reps = n // LANES
    return x * jnp.concatenate([cos] * reps, axis=1) + rot * jnp.concatenate([sin_signed] * reps, axis=1)


def _rope_cols(xT, cosT, sinT):
    half = HEAD_DIM // 2
    out = []
    for h in range(xT.shape[0] // HEAD_DIM):
        x1 = xT[h * HEAD_DIM:h * HEAD_DIM + half]
        x2 = xT[h * HEAD_DIM + half:(h + 1) * HEAD_DIM]
        out += [x1 * cosT - x2 * sinT, x2 * cosT + x1 * sinT]
    return jnp.concatenate(out, axis=0)


def _swa_proj_kernel(x_ref, g_ref, wq_ref, wkvT_ref, wkv_ref, cos_ref, sin_ref, cosT_ref, sinT_ref,
                     q_ref, kT_ref, vT_ref, kTb_ref, vb_ref, kn_ref, vn_ref, *, q_scale):
    h = _rms_bf16(x_ref[0], g_ref[...])
    q = jnp.dot(h, wq_ref[...], preferred_element_type=F32)
    q_ref[0] = (_rope_rows(q, cos_ref[...], sin_ref[...]) * q_scale).astype(BF16)
    kv = lax.dot_general(wkvT_ref[...], h, _NT, preferred_element_type=F32)
    nk = N_KV_SWA * HEAD_DIM
    k = _rope_cols(kv[:nk], cosT_ref[...], sinT_ref[...])
    kT_ref[0] = k
    vT_ref[0] = kv[nk:]
    kTb_ref[0] = k.astype(BF16)
    kvn = jnp.dot(h, wkv_ref[...], preferred_element_type=F32)
    kn_ref[0] = _rope_rows(kvn[:, :nk], cos_ref[...], sin_ref[...])
    vn_ref[0] = kvn[:, nk:]
    vb_ref[0] = kvn[:, nk:].astype(BF16)


def _swa_proj(x, g, wq, wkvT, wkv, pos, tm, q_scale):
    B, S, D = x.shape
    tm = min(tm, S)
    half = HEAD_DIM // 2
    inv = ROPE_THETA ** (-jnp.arange(half, dtype=F32) / half)
    ang = pos.astype(F32)[:, None] * inv[None, :]
    cos, sin = jnp.cos(ang), jnp.sin(ang)
    cos_rows = jnp.concatenate([cos] * (LANES // half), axis=1)
    sin_rows = jnp.concatenate([-sin, sin] * (LANES // HEAD_DIM), axis=1)
    nk = N_KV_SWA * HEAD_DIM
    const = lambda shape: pl.BlockSpec(shape, lambda b, t: (0,) * len(shape))
    tspec = pl.BlockSpec((1, nk, tm), lambda b, t: (b, 0, t))
    rspec = pl.BlockSpec((1, tm, nk), lambda b, t: (b, t, 0))
    assert nk == LANES
    return pl.pallas_call(
        functools.partial(_swa_proj_kernel, q_scale=q_scale),
        grid=(B, S // tm),
        in_specs=[pl.BlockSpec((1, tm, D), lambda b, t: (b, t, 0)), const(g.shape), const(wq.shape),
                  const(wkvT.shape), const(wkv.shape),
                  pl.BlockSpec((tm, LANES), lambda b, t: (t, 0)), pl.BlockSpec((tm, LANES), lambda b, t: (t, 0)),
                  pl.BlockSpec((half, tm), lambda b, t: (0, t)), pl.BlockSpec((half, tm), lambda b, t: (0, t))],
        out_specs=[pl.BlockSpec((1, tm, N_HEADS * HEAD_DIM), lambda b, t: (b, t, 0)),
                   tspec, tspec, tspec, rspec, rspec, rspec],
        out_shape=[jax.ShapeDtypeStruct((B, S, N_HEADS * HEAD_DIM), BF16),
                   jax.ShapeDtypeStruct((B, nk, S), F32), jax.ShapeDtypeStruct((B, nk, S), F32),
                   jax.ShapeDtypeStruct((B, nk, S), BF16), jax.ShapeDtypeStruct((B, S, nk), BF16),
                   jax.ShapeDtypeStruct((B, S, nk), F32), jax.ShapeDtypeStruct((B, S, nk), F32)],
        compiler_params=_cparams("arbitrary", "arbitrary"),
        name="swa_proj",
    )(x, g, wq, wkvT, wkv, cos_rows, sin_rows, cos.T, sin.T)


def _swa_prompt_attn_kernel(sink_ref, q_ref, kT_ref, vb_ref, o_ref, *, tq):
    qi = pl.program_id(1)
    G = N_HEADS // N_KV_SWA
    W = WINDOW
    r = lax.broadcasted_iota(jnp.int32, (G * W, 2 * W), 0) % W
    c = lax.broadcasted_iota(jnp.int32, (G * W, 2 * W), 1)
    ones = jnp.ones((2 * W, LANES), BF16)
    for n in range(tq // W):
        blk = qi * (tq // W) + n
        k0 = pl.multiple_of(jnp.maximum(blk - 1, 0) * W, W)
        delta = (blk * W + r) - (k0 + c)
        mask = jnp.where(delta >= 0, delta, W + 1) <= W
        outs = []
        for kv in range(N_KV_SWA):
            q8 = jnp.concatenate(
                [q_ref[0, n * W:(n + 1) * W, (kv * G + g) * HEAD_DIM:(kv * G + g + 1) * HEAD_DIM]
                 for g in range(G)], axis=0)
            sink = jnp.concatenate(
                [jnp.full((W, LANES), sink_ref[kv * G + g] * LOG2E, F32) for g in range(G)], axis=0)
            kT = kT_ref[0, kv * HEAD_DIM:(kv + 1) * HEAD_DIM, pl.ds(k0, 2 * W)]
            v = vb_ref[0, pl.ds(k0, 2 * W), kv * HEAD_DIM:(kv + 1) * HEAD_DIM]
            s = jnp.where(mask, jnp.dot(q8, kT, preferred_element_type=F32), NEG_INF)
            m = jnp.maximum(jnp.max(s, axis=-1, keepdims=True), sink)
            p = jnp.exp2(s - jnp.concatenate([m] * (2 * W // LANES), axis=1)).astype(BF16)
            den = jnp.dot(p, ones, preferred_element_type=F32) + jnp.exp2(sink - m)
            o8 = jnp.dot(p, v, preferred_element_type=F32) / den[:, :HEAD_DIM]
            outs += [o8[g * W:(g + 1) * W] for g in range(G)]
        o_ref[0, n * W:(n + 1) * W, :] = jnp.concatenate(outs, axis=1).astype(BF16)


def _swa_prompt_attn(sinks, q, kTb, vb, tq):
    B, S, _ = q.shape
    assert S % WINDOW == 0 and S >= 2 * WINDOW
    tq = min(tq, S)
    nk = N_KV_SWA * HEAD_DIM
    grid_spec = pltpu.PrefetchScalarGridSpec(
        num_scalar_prefetch=1,
        grid=(B, S // tq),
        in_specs=[pl.BlockSpec((1, tq, N_HEADS * HEAD_DIM), lambda b, i, sk: (b, i, 0)),
                  pl.BlockSpec((1, nk, S), lambda b, i, sk: (b, 0, 0)),
                  pl.BlockSpec((1, S, nk), lambda b, i, sk: (b, 0, 0))],
        out_specs=pl.BlockSpec((1, tq, N_HEADS * HEAD_DIM), lambda b, i, sk: (b, i, 0)),
    )
    return pl.pallas_call(
        functools.partial(_swa_prompt_attn_kernel, tq=tq),
        grid_spec=grid_spec,
        out_shape=jax.ShapeDtypeStruct((B, S, N_HEADS * HEAD_DIM), BF16),
        compiler_params=_cparams("arbitrary", "arbitrary"),
        name="swa_prompt_attn",
    )(sinks, q, kTb, vb)


def _swa_decode_kernel(q_ref, kc_ref, vc_ref, kn_ref, vn_ref, knT_ref, vnT_ref, sink_ref,
                       o_ref, ko_ref, vo_ref):
    G = N_HEADS // N_KV_SWA
    nk = N_KV_SWA * HEAD_DIM
    bs, _, wb = kc_ref.shape
    sink = sink_ref[...]
    rh = lax.broadcasted_iota(jnp.int32, (N_HEADS, HEAD_DIM), 0) // G
    lane = lax.broadcasted_iota(jnp.int32, (nk, wb), 1)
    seq_lane = lax.broadcasted_iota(jnp.int32, knT_ref.shape, 1)
    for i in range(bs):
        s_idx = pl.program_id(0) * bs + i
        q16 = q_ref[i].astype(F32)
        qt = jnp.concatenate([q16] * N_KV_SWA, axis=1)
        rr = lax.broadcasted_iota(jnp.int32, qt.shape, 0) // G
        cc = lax.broadcasted_iota(jnp.int32, qt.shape, 1) // HEAD_DIM
        qbd = jnp.where(rr == cc, qt, 0.0)
        kT = kc_ref[i]
        vT = vc_ref[i]
        s_p = jnp.dot(qbd.astype(BF16), kT.astype(BF16), preferred_element_type=F32)
        s_n = jnp.sum(qbd.astype(BF16).astype(F32) * kn_ref[i].astype(BF16).astype(F32), axis=-1, keepdims=True)
        m = jnp.maximum(jnp.maximum(jnp.max(s_p, axis=-1, keepdims=True), s_n), sink)
        p_p = jnp.exp(s_p - m)
        p_n = jnp.exp(s_n - m)
        den = jnp.sum(p_p, axis=-1, keepdims=True) + p_n + jnp.exp(sink - m)
        acc = lax.dot_general((p_p / den).astype(BF16), vT.astype(BF16), _NT, preferred_element_type=F32)
        acc = acc + (p_n / den).astype(BF16).astype(F32) * vn_ref[i].astype(BF16).astype(F32)
        o = jnp.zeros((N_HEADS, HEAD_DIM), F32)
        for kv in range(N_KV_SWA):
            o = jnp.where(rh == kv, acc[:, kv * HEAD_DIM:(kv + 1) * HEAD_DIM], o)
        o_ref[i] = o.astype(BF16)
        k_col = jnp.sum(jnp.where(seq_lane == s_idx, knT_ref[...], 0.0), axis=-1, keepdims=True)
        v_col = jnp.sum(jnp.where(seq_lane == s_idx, vnT_ref[...], 0.0), axis=-1, keepdims=True)
        ko_ref[i] = jnp.where(lane == wb - 1, k_col, pltpu.roll(kT, wb - 1, 1))
        vo_ref[i] = jnp.where(lane == wb - 1, v_col, pltpu.roll(vT, wb - 1, 1))


def _swa_decode_attn(q, kc, vc, k_new, v_new, k_newT, v_newT, sinks, bs):
    N, nk, wb = kc.shape
    assert wb == LANES and wb <= WINDOW
    bs = math.gcd(bs, N)
    seq3 = lambda shape: pl.BlockSpec((bs,) + shape, lambda s: (s, 0, 0))
    const = lambda shape: pl.BlockSpec(shape, lambda s: (0,) * len(shape))
    return pl.pallas_call(
        _swa_decode_kernel,
        grid=(N // bs,),
        in_specs=[seq3((N_HEADS, HEAD_DIM)), seq3((nk, wb)), seq3((nk, wb)), seq3((1, nk)), seq3((1, nk)),
                  const(k_newT.shape), const(v_newT.shape), const(sinks.shape)],
        out_specs=[seq3((N_HEADS, HEAD_DIM)), seq3((nk, wb)), seq3((nk, wb))],
        out_shape=[jax.ShapeDtypeStruct((N, N_HEADS, HEAD_DIM), BF16),
                   jax.ShapeDtypeStruct((N, nk, wb), F32), jax.ShapeDtypeStruct((N, nk, wb), F32)],
        compiler_params=_cparams("arbitrary"),
        name="swa_decode_attn",
    )(q, kc, vc, k_new, v_new, k_newT, v_newT, sinks)


def _silu(x):
    return x * (1.0 / (1.0 + jnp.exp(-x)))


def _post_prompt_kernel(x_ref, o_ref, wo_ref, g_ref, win_ref, cw_ref, cb_ref, wout_ref, gf_ref,
                        y_ref, ulast_ref, halo_ref, ubuf_ref, *, n_chunks, final_norm):
    ti = pl.program_id(1)
    tm = x_ref.shape[1]
    F = wout_ref.shape[0]
    fc = F // n_chunks
    pad = ubuf_ref.shape[0] - tm

    @pl.when(ti == 0)
    def _():
        halo_ref[...] = jnp.zeros_like(halo_ref)

    xn = x_ref[0] + jnp.dot(o_ref[0], wo_ref[...], preferred_element_type=F32)
    h = _rms_bf16(xn, g_ref[...])
    y = jnp.zeros_like(xn)
    for ci in range(n_chunks):
        halves = []
        for c0 in (ci * fc, F + ci * fc):
            u = jnp.dot(h, win_ref[:, c0:c0 + fc], preferred_element_type=F32)
            ubuf_ref[0:pad, :] = halo_ref[:, c0:c0 + fc]
            ubuf_ref[pad:pad + tm, :] = u
            halo_ref[:, c0:c0 + fc] = u[tm - pad:]
            ulast_ref[0, :, c0:c0 + fc] = u[tm - pad:]
            conv = (cb_ref[:, c0:c0 + fc] + cw_ref[0:1, c0:c0 + fc] * ubuf_ref[pad - 2:pad - 2 + tm, :]
                    + cw_ref[1:2, c0:c0 + fc] * ubuf_ref[pad - 1:pad - 1 + tm, :]
                    + cw_ref[2:3, c0:c0 + fc] * u)
            halves.append(conv)
        a = (_silu(halves[0]) * halves[1]).astype(BF16)
        y = y + jnp.dot(a, wout_ref[ci * fc:(ci + 1) * fc, :], preferred_element_type=F32)
    out = xn + y
    if final_norm:
        out = out * lax.rsqrt(jnp.mean(out * out, axis=-1, keepdims=True) + RMS_EPS) * gf_ref[...]
    y_ref[0] = out


def _post_prompt(x, o, wo, g, win, cw, cb, wout, gf, final_norm, tm, n_chunks):
    B, S, D = x.shape
    tm = min(tm, S)
    F2 = win.shape[1]
    pad = 8
    const = lambda shape: pl.BlockSpec(shape, lambda b, t: (0,) * len(shape))
    row = pl.BlockSpec((1, tm, D), lambda b, t: (b, t, 0))
    return pl.pallas_call(
        functools.partial(_post_prompt_kernel, n_chunks=n_chunks, final_norm=final_norm),
        grid=(B, S // tm),
        in_specs=[row, row, const(wo.shape), const(g.shape), const(win.shape), const(cw.shape),
                  const(cb.shape), const(wout.shape), const(gf.shape)],
        out_specs=[row, pl.BlockSpec((1, pad, F2), lambda b, t: (b, 0, 0))],
        out_shape=[jax.ShapeDtypeStruct((B, S, D), F32), jax.ShapeDtypeStruct((B, pad, F2), F32)],
        scratch_shapes=[pltpu.VMEM((pad, F2), F32), pltpu.VMEM((tm + pad, F2 // (2 * n_chunks)), F32)],
        compiler_params=_cparams("arbitrary", "arbitrary"),
        name="post_prompt",
    )(x, o, wo, g, win, cw, cb, wout, gf)


def _post_sample_kernel(x_ref, o_ref, wo_ref, g_ref, win_ref, cw_ref, cb_ref, wout_ref, gf_ref,
                        s0_ref, s1_ref, y_ref, u_ref, *, n_chunks, final_norm):
    F = wout_ref.shape[0]
    fc = F // n_chunks
    xn = x_ref[...] + jnp.dot(o_ref[...], wo_ref[...], preferred_element_type=F32)
    h = _rms_bf16(xn, g_ref[...])
    y = jnp.zeros_like(xn)
    for ci in range(n_chunks):
        halves = []
        for c0 in (ci * fc, F + ci * fc):
            u = jnp.dot(h, win_ref[:, c0:c0 + fc], preferred_element_type=F32)
            u_ref[:, c0:c0 + fc] = u
            halves.append(cb_ref[:, c0:c0 + fc] + cw_ref[0:1, c0:c0 + fc] * s0_ref[:, c0:c0 + fc]
                          + cw_ref[1:2, c0:c0 + fc] * s1_ref[:, c0:c0 + fc] + cw_ref[2:3, c0:c0 + fc] * u)
        a = (_silu(halves[0]) * halves[1]).astype(BF16)
        y = y + jnp.dot(a, wout_ref[ci * fc:(ci + 1) * fc, :], preferred_element_type=F32)
    out = xn + y
    if final_norm:
        out = out * lax.rsqrt(jnp.mean(out * out, axis=-1, keepdims=True) + RMS_EPS) * gf_ref[...]
    y_ref[...] = out


def _post_sample(x, o, wo, g, win, cw, cb, wout, gf, s0, s1, final_norm, n_chunks):
    N, D = x.shape
    F2 = win.shape[1]
    return pl.pallas_call(
        functools.partial(_post_sample_kernel, n_chunks=n_chunks, final_norm=final_norm),
        out_shape=[jax.ShapeDtypeStruct((N, D), F32), jax.ShapeDtypeStruct((N, F2), F32)],
        compiler_params=_cparams(),
        name="post_sample",
    )(x, o, wo, g, win, cw, cb, wout, gf, s0, s1)


PROJ_TILE = 512
POST_TILE = 512
FFN_CHUNKS = 2
FOX_TQ = 512
FOX_KEY_SPAN = 4
SWA_TQ = 512
DECODE_PAGES_PER_BLOCK = 16
SWA_DECODE_SEQS = 8


def kernel(x_prompt, x_sample, cache_fox_k, cache_fox_v, cache_fox_logf, cache_swa_k, cache_swa_v, state_ffn_conv, page_table, norm_mixer, norm_ffn, norm_final, fox_w_qkv, fox_b_f, fox_w_o, swa_w_qkv, swa_sinks, swa_w_o, ffn_w_in, ffn_conv_w, ffn_conv_b, ffn_w_out):
    B, S, D = x_prompt.shape
    N = x_sample.shape[0]
    assert x_sample.shape[1] == 1
    depth = norm_mixer.shape[0]
    n_phys, page = cache_fox_k.shape[1], cache_fox_k.shape[2]
    past_len = page_table.shape[1] * page
    nq = N_HEADS * HEAD_DIM
    nkf = N_KV_FOX * HEAD_DIM
    nks = N_KV_SWA * HEAD_DIM
    F = ffn_w_out.shape[1]

    xp = x_prompt
    xs = x_sample.reshape(1, N, D)
    gfin = norm_final.reshape(1, D)

    kc = jnp.transpose(cache_fox_k, (0, 1, 3, 4, 2)).reshape(-1, nkf, page)
    vc = jnp.transpose(cache_fox_v, (0, 1, 3, 4, 2)).reshape(-1, nkf, page)
    lc = jnp.transpose(cache_fox_logf, (0, 1, 3, 2)).reshape(-1, N_HEADS, page)

    fkp, fvp, flp, fks, fvs, fls = [], [], [], [], [], []
    skp, svp, sks, svs = [], [], [], []
    cps, css = [], []
    for i in range(depth):
        j = i // 2
        g_mix = norm_mixer[i].reshape(1, D)
        if i % 2 == 0:
            w = fox_w_qkv[j]
            wq = w[:, :nq].astype(BF16)
            wkvT = w[:, nq:nq + 2 * nkf].T.astype(BF16)
            wva = jnp.pad(w[:, nq + nkf:nq + 2 * nkf].reshape(D, N_KV_FOX, HEAD_DIM),
                          ((0, 0), (0, 0), (0, LANES - HEAD_DIM))).reshape(D, N_KV_FOX * LANES).astype(BF16)
            wfT = w[:, nq + 2 * nkf:].T.astype(BF16)
            bf = fox_b_f[j].reshape(N_HEADS, 1)
            q, kT, vT, kTa, va, lfT = _fox_proj(xp, g_mix, wq, wkvT, wva, wfT, bf, PROJ_TILE, SCALE * LOG2E)
            op = _fox_prompt_attn(q, kTa, va, FOX_TQ, FOX_KEY_SPAN)
            fkp.append(jnp.transpose(kT, (0, 3, 1, 2)))
            fvp.append(jnp.transpose(vT, (0, 3, 1, 2)))
            flp.append(jnp.transpose(lfT, (0, 2, 1)))

            qs, kTs, vTs, _, _, lfTs = _fox_proj(xs, g_mix, wq, wkvT, wva, wfT, bf, PROJ_TILE, SCALE)
            k_new = jnp.transpose(kTs[0], (2, 0, 1)).reshape(N, 1, nkf)
            v_new = jnp.transpose(vTs[0], (2, 0, 1)).reshape(N, 1, nkf)
            lf_new = jnp.transpose(lfTs[0], (1, 0)).reshape(N, N_HEADS, 1)
            os_ = _fox_decode_attn(page_table, qs.reshape(N, N_HEADS, HEAD_DIM), k_new, v_new, lf_new,
                                   kc, vc, lc, j * n_phys, DECODE_PAGES_PER_BLOCK)
            os_ = os_.reshape(N, nq)
            fks.append(jnp.transpose(kTs[0], (2, 0, 1)).reshape(N, 1, N_KV_FOX, HEAD_DIM))
            fvs.append(jnp.transpose(vTs[0], (2, 0, 1)).reshape(N, 1, N_KV_FOX, HEAD_DIM))
            fls.append(jnp.transpose(lfTs[0], (1, 0)).reshape(N, 1, N_HEADS))
            wo = fox_w_o[j].astype(BF16)
        else:
            w = swa_w_qkv[j]
            wq = w[:, :nq].astype(BF16)
            wkv = w[:, nq:].astype(BF16)
            wkvT = w[:, nq:].T.astype(BF16)
            sinks = swa_sinks[j]
            q, kT, vT, kTb, vb, _, _ = _swa_proj(xp, g_mix, wq, wkvT, wkv, jnp.arange(S), PROJ_TILE,
                                                 SCALE * LOG2E)
            op = _swa_prompt_attn(sinks, q, kTb, vb, SWA_TQ)
            keep = min(WINDOW, S)
            skp.append(jnp.transpose(kT[:, :, S - keep:].reshape(B, N_KV_SWA, HEAD_DIM, keep), (0, 3, 1, 2)))
            svp.append(jnp.transpose(vT[:, :, S - keep:].reshape(B, N_KV_SWA, HEAD_DIM, keep), (0, 3, 1, 2)))

            pos_s = jnp.full((N,), past_len, jnp.int32)
            qs, kTs, vTs, _, _, kns, vns = _swa_proj(xs, g_mix, wq, wkvT, wkv, pos_s, PROJ_TILE, SCALE)
            wb = cache_swa_k.shape[2]
            kcs = jnp.transpose(cache_swa_k[j], (0, 2, 3, 1)).reshape(N, nks, wb)
            vcs = jnp.transpose(cache_swa_v[j], (0, 2, 3, 1)).reshape(N, nks, wb)
            os_, ko, vo = _swa_decode_attn(qs.reshape(N, N_HEADS, HEAD_DIM), kcs, vcs,
                                           kns.reshape(N, 1, nks), vns.reshape(N, 1, nks),
                                           kTs[0], vTs[0], sinks.reshape(N_HEADS, 1), SWA_DECODE_SEQS)
            os_ = os_.reshape(N, nq)
            sks.append(jnp.transpose(ko.reshape(N, N_KV_SWA, HEAD_DIM, wb), (0, 3, 1, 2)))
            svs.append(jnp.transpose(vo.reshape(N, N_KV_SWA, HEAD_DIM, wb), (0, 3, 1, 2)))
            wo = swa_w_o[j].astype(BF16)

        g_ffn = norm_ffn[i].reshape(1, D)
        win = ffn_w_in[i].astype(BF16)
        wout = ffn_w_out[i].astype(BF16)
        cw = ffn_conv_w[i]
        cb = ffn_conv_b[i].reshape(1, 2 * F)
        last = i == depth - 1
        xp, ulast = _post_prompt(xp, op, wo, g_ffn, win, cw, cb, wout, gfin, last, POST_TILE, FFN_CHUNKS)
        cps.append(ulast[:, -(CONV_W - 1):])
        st = state_ffn_conv[i]
        xs2, us = _post_sample(xs[0], os_, wo, g_ffn, win, cw, cb, wout, gfin, st[:, 0], st[:, 1],
                               last, FFN_CHUNKS)
        xs = xs2.reshape(1, N, D)
        css.append(jnp.stack([st[:, 1], us], axis=1))

    return (xp, xs.reshape(N, 1, D),
            jnp.stack(fkp), jnp.stack(fvp), jnp.stack(flp),
            jnp.stack(fks), jnp.stack(fvs), jnp.stack(fls),
            jnp.stack(skp), jnp.stack(svp), jnp.stack(sks), jnp.stack(svs),
            jnp.stack(cps), jnp.stack(css))
```

```python
import functools
import math

import jax
import jax.numpy as jnp
from jax import lax
from jax.experimental import pallas as pl
from jax.experimental.pallas import tpu as pltpu

HEAD_DIM = 64
N_HEADS = 16
N_KV_FOX = 8
N_KV_SWA = 2
WINDOW = 128
ROPE_THETA = 10000.0
RMS_EPS = 1e-6
CONV_W = 3
SCALE = HEAD_DIM ** -0.5
LOG2E = math.log2(math.e)

F32 = jnp.float32
BF16 = jnp.bfloat16

V7X_VMEM_LIMIT_BYTES = 56 * 1024 * 1024
V7X_MXU_DIM = 256
LANES = 128
NEG_INF = float("-inf")

_NT = (((1,), (1,)), ((), ()))


def _cparams(*sem):
    return pltpu.CompilerParams(dimension_semantics=sem, vmem_limit_bytes=V7X_VMEM_LIMIT_BYTES)


def _rms_bf16(x, g):
    y = x * lax.rsqrt(jnp.mean(x * x, axis=-1, keepdims=True) + RMS_EPS)
    return (y * g).astype(BF16)


def _split3(x):
    hi = x.astype(BF16)
    r1 = x - hi.astype(F32)
    mid = r1.astype(BF16)
    lo = (r1 - mid.astype(F32)).astype(BF16)
    return hi, mid, lo


def _log_sigmoid(x):
    return jnp.minimum(x, 0.0) - jnp.log1p(jnp.exp(-jnp.abs(x)))


def _upper_tri(n):
    r = lax.broadcasted_iota(jnp.int32, (n, n), 0)
    c = lax.broadcasted_iota(jnp.int32, (n, n), 1)
    return (r <= c).astype(BF16)


AUG_ROWS = 16
N_SPLIT = 3


def _bias_select():
    G = N_HEADS // N_KV_FOX
    r = lax.broadcasted_iota(jnp.int32, (N_KV_FOX * AUG_ROWS, N_SPLIT * N_HEADS), 0)
    c = lax.broadcasted_iota(jnp.int32, (N_KV_FOX * AUG_ROWS, N_SPLIT * N_HEADS), 1)
    kv, rr = r // AUG_ROWS, r % AUG_ROWS
    g, part = rr // N_SPLIT, rr % N_SPLIT
    return ((rr < G * N_SPLIT) & (c == part * N_HEADS + kv * G + g)).astype(BF16)


def _fox_proj_kernel(x_ref, g_ref, wq_ref, wkvT_ref, wv_ref, wfT_ref, bf_ref, tri_ref, sel_ref,
                     q_ref, kT_ref, vT_ref, kTa_ref, va_ref, lfT_ref, carry_ref, *, q_scale):
    ti = pl.program_id(1)
    tm = x_ref.shape[1]
    h = _rms_bf16(x_ref[0], g_ref[...])
    q = jnp.dot(h, wq_ref[...], preferred_element_type=F32)
    q_ref[0] = (q * q_scale).astype(BF16)
    kv = lax.dot_general(wkvT_ref[...], h, _NT, preferred_element_type=F32)
    nk = N_KV_FOX * HEAD_DIM
    kT_ref[0] = kv[:nk].reshape(kT_ref.shape[1:])
    vT_ref[0] = kv[nk:].reshape(vT_ref.shape[1:])
    f = lax.dot_general(wfT_ref[...], h, _NT, preferred_element_type=F32)
    lf = _log_sigmoid(f + bf_ref[...])
    lfT_ref[0] = lf

    @pl.when(ti == 0)
    def _():
        carry_ref[...] = jnp.zeros_like(carry_ref)

    hi, mid, lo = _split3(lf)
    tri = tri_ref[...]
    c = (jnp.dot(hi, tri, preferred_element_type=F32) + jnp.dot(mid, tri, preferred_element_type=F32)
         + jnp.dot(lo, tri, preferred_element_type=F32)) + carry_ref[:, 0:1]
    carry_ref[...] = jnp.broadcast_to(c[:, -1:], carry_ref.shape)

    parts = jnp.concatenate(_split3(c * (-LOG2E)), axis=0)
    aug = jnp.dot(sel_ref[...], parts, preferred_element_type=F32).astype(BF16)
    zeros = jnp.zeros((LANES - HEAD_DIM - AUG_ROWS, tm), BF16)
    for kvh in range(N_KV_FOX):
        kTa_ref[0, kvh] = jnp.concatenate(
            [kv[kvh * HEAD_DIM:(kvh + 1) * HEAD_DIM].astype(BF16), aug[kvh * AUG_ROWS:(kvh + 1) * AUG_ROWS], zeros],
            axis=0)
    vn = jnp.dot(h, wv_ref[...], preferred_element_type=F32)
    lane = lax.broadcasted_iota(jnp.int32, (tm, LANES), 1)
    per_block = LANES // HEAD_DIM
    pieces = []
    for kvh in range(N_KV_FOX):
        blk = vn[:, kvh // per_block * LANES:(kvh // per_block + 1) * LANES]
        shift = kvh % per_block * HEAD_DIM
        pieces.append(jnp.where(lane < HEAD_DIM, blk if shift == 0 else pltpu.roll(blk, LANES - shift, 1), 1.0))
    va_ref[0] = jnp.concatenate(pieces, axis=1).astype(BF16)


def _fox_proj(x, g, wq, wkvT, wv, wfT, bf, tm, q_scale):
    B, S, D = x.shape
    tm = min(tm, S)
    nt = S // tm
    kvshape = (B, N_KV_FOX, HEAD_DIM, S)
    kvblock = pl.BlockSpec((1, N_KV_FOX, HEAD_DIM, tm), lambda b, t: (b, 0, 0, t))
    hblock = pl.BlockSpec((1, N_HEADS, tm), lambda b, t: (b, 0, t))
    const = lambda shape: pl.BlockSpec(shape, lambda b, t: (0,) * len(shape))
    sel = _bias_select()
    return pl.pallas_call(
        functools.partial(_fox_proj_kernel, q_scale=q_scale),
        grid=(B, nt),
        in_specs=[pl.BlockSpec((1, tm, D), lambda b, t: (b, t, 0)), const(g.shape), const(wq.shape),
                  const(wkvT.shape), const(wv.shape), const(wfT.shape), const(bf.shape), const((tm, tm)),
                  const(sel.shape)],
        out_specs=[pl.BlockSpec((1, tm, N_HEADS * HEAD_DIM), lambda b, t: (b, t, 0)),
                   kvblock, kvblock,
                   pl.BlockSpec((1, N_KV_FOX, LANES, tm), lambda b, t: (b, 0, 0, t)),
                   pl.BlockSpec((1, tm, N_KV_FOX * LANES), lambda b, t: (b, t, 0)), hblock],
        out_shape=[jax.ShapeDtypeStruct((B, S, N_HEADS * HEAD_DIM), BF16),
                   jax.ShapeDtypeStruct(kvshape, F32), jax.ShapeDtypeStruct(kvshape, F32),
                   jax.ShapeDtypeStruct((B, N_KV_FOX, LANES, S), BF16),
                   jax.ShapeDtypeStruct((B, S, N_KV_FOX * LANES), BF16),
                   jax.ShapeDtypeStruct((B, N_HEADS, S), F32)],
        scratch_shapes=[pltpu.VMEM((N_HEADS, LANES), F32)],
        compiler_params=_cparams("arbitrary", "arbitrary"),
        name="fox_proj",
    )(x, g, wq, wkvT, wv, wfT, bf, _upper_tri(tm), sel)


def _fox_prompt_attn_kernel(q_ref, kTa_ref, va_ref, o_ref, qa_ref, m_ref, acc_ref, *, tq, span):
    qi = pl.program_id(2)
    G = N_HEADS // N_KV_FOX
    lane = lax.broadcasted_iota(jnp.int32, (tq, LANES), 1)
    qf = q_ref[0].astype(F32)
    for g in range(G):
        lo = HEAD_DIM + g * N_SPLIT
        ones = jnp.where((lane >= lo) & (lane < lo + N_SPLIT), 1.0, 0.0)
        qg = qf if g == 0 else pltpu.roll(qf, LANES - g * HEAD_DIM, 1)
        qa_ref[g * tq:(g + 1) * tq, :] = jnp.where(lane < HEAD_DIM, qg, ones).astype(BF16)
    m_ref[...] = jnp.full_like(m_ref, NEG_INF)
    acc_ref[...] = jnp.zeros_like(acc_ref)

    def chunk(blk0, nblk, diag):
        tk = nblk * tq
        k0 = pl.multiple_of(blk0 * tq, tk)
        kTa = kTa_ref[0, 0, :, pl.ds(k0, tk)]
        va = va_ref[0, pl.ds(k0, tk), :]
        s = jnp.dot(qa_ref[...], kTa, preferred_element_type=F32)
        if diag:
            row = lax.broadcasted_iota(jnp.int32, s.shape, 0) % tq
            col = lax.broadcasted_iota(jnp.int32, s.shape, 1)
            s = jnp.where(col <= row, s, NEG_INF)
        m_prev = m_ref[...]
        m_new = jnp.maximum(m_prev, jnp.max(s, axis=-1, keepdims=True))
        p = jnp.exp2(s - jnp.concatenate([m_new] * (tk // LANES), axis=1))
        alpha = jnp.exp2(m_prev - m_new)
        acc_ref[...] = alpha * acc_ref[...] + jnp.dot(p.astype(BF16), va, preferred_element_type=F32)
        m_ref[...] = m_new

    def body(i, carry):
        chunk(i * span, span, False)
        return carry

    lax.fori_loop(0, qi // span, body, 0)
    size = span // 2
    while size >= 1:
        @pl.when(qi % (2 * size) >= size)
        def _(size=size):
            chunk(qi // (2 * size) * (2 * size), size, False)
        size //= 2
    chunk(qi, 1, True)
    acc = acc_ref[...]
    o = acc[:, :HEAD_DIM] / acc[:, HEAD_DIM:]
    o_ref[0] = jnp.concatenate([o[g * tq:(g + 1) * tq] for g in range(G)], axis=1).astype(BF16)


def _fox_prompt_attn(q, kTa, va, tq, span):
    B, S, _ = q.shape
    tq = min(tq, S)
    G = N_HEADS // N_KV_FOX
    assert G * HEAD_DIM == LANES and HEAD_DIM + G * N_SPLIT <= HEAD_DIM + AUG_ROWS
    assert tq % LANES == 0 and S % tq == 0
    return pl.pallas_call(
        functools.partial(_fox_prompt_attn_kernel, tq=tq, span=span),
        grid=(B, N_KV_FOX, S // tq),
        in_specs=[pl.BlockSpec((1, tq, LANES), lambda b, h, i: (b, i, h)),
                  pl.BlockSpec((1, 1, LANES, S), lambda b, h, i: (b, h, 0, 0)),
                  pl.BlockSpec((1, S, LANES), lambda b, h, i: (b, 0, h))],
        out_specs=pl.BlockSpec((1, tq, LANES), lambda b, h, i: (b, i, h)),
        out_shape=jax.ShapeDtypeStruct((B, S, N_HEADS * HEAD_DIM), BF16),
        scratch_shapes=[pltpu.VMEM((G * tq, LANES), BF16), pltpu.VMEM((G * tq, LANES), F32),
                        pltpu.VMEM((G * tq, LANES), F32)],
        compiler_params=_cparams("arbitrary", "arbitrary", "arbitrary"),
        name="fox_prompt_attn",
    )(q, kTa, va)


def _fox_decode_kernel(pt_ref, q_ref, kn_ref, vn_ref, lfn_ref, tri_ref, kc_hbm, vc_hbm, lc_hbm,
                       o_ref, kbuf, vbuf, lbuf, sems, qbd_ref, m_ref, l_ref, acc_ref, carry_ref,
                       *, pb, nblk, page_base):
    s_idx = pl.program_id(0)
    j = pl.program_id(1)
    t = s_idx * nblk + j
    n_steps = pl.num_programs(0) * nblk
    slot = lax.rem(t, 2)
    page = LANES
    nh = N_HEADS

    def copies(seq, blk, sl):
        out = []
        for i in range(pb):
            pid = pt_ref[seq, blk * pb + i] + page_base
            out.append(pltpu.make_async_copy(kc_hbm.at[pid], kbuf.at[sl, :, pl.ds(i * page, page)], sems.at[sl, 0]))
            out.append(pltpu.make_async_copy(vc_hbm.at[pid], vbuf.at[sl, :, pl.ds(i * page, page)], sems.at[sl, 1]))
            out.append(pltpu.make_async_copy(lc_hbm.at[pid], lbuf.at[sl, pl.ds(i * nh, nh), :], sems.at[sl, 2]))
        return out

    @pl.when(t == 0)
    def _():
        for cp in copies(s_idx, j, slot):
            cp.start()

    @pl.when(t + 1 < n_steps)
    def _():
        last = j + 1 == nblk
        for cp in copies(jnp.where(last, s_idx + 1, s_idx), jnp.where(last, 0, j + 1), 1 - slot):
            cp.start()

    @pl.when(j == 0)
    def _():
        q16 = q_ref[0].astype(F32)
        qt = jnp.concatenate([q16] * N_KV_FOX, axis=1)
        r = lax.broadcasted_iota(jnp.int32, qt.shape, 0) // (N_HEADS // N_KV_FOX)
        c = lax.broadcasted_iota(jnp.int32, qt.shape, 1) // HEAD_DIM
        qbd_ref[...] = jnp.where(r == c, qt, 0.0).astype(BF16)
        m_ref[...] = jnp.full_like(m_ref, NEG_INF)
        l_ref[...] = jnp.zeros_like(l_ref)
        acc_ref[...] = jnp.zeros_like(acc_ref)
        carry_ref[...] = jnp.zeros_like(carry_ref)

    for cp in copies(s_idx, j, slot):
        cp.wait()

    qbd = qbd_ref[...]
    x = lbuf[slot]
    n = pb * nh
    w3 = jnp.dot(jnp.concatenate(_split3(x), axis=0), tri_ref[...], preferred_element_type=F32)
    w = w3[:n] + w3[n:2 * n] + w3[2 * n:]
    off = carry_ref[:, 0:1]
    cs = []
    for i in range(pb):
        wi = w[i * nh:(i + 1) * nh]
        cs.append(wi + off)
        off = off + wi[:, page - 1:page]
    carry_ref[...] = jnp.broadcast_to(off, carry_ref.shape)
    bias = jnp.concatenate(cs, axis=1)

    s = jnp.dot(qbd, kbuf[slot].astype(BF16), preferred_element_type=F32) - bias
    m_prev = m_ref[...]
    m_new = jnp.maximum(m_prev, jnp.max(s, axis=-1, keepdims=True))
    p = jnp.exp(s - m_new)
    alpha = jnp.exp(m_prev - m_new)
    l_ref[...] = alpha * l_ref[...] + jnp.sum(p, axis=-1, keepdims=True)
    acc_ref[...] = alpha * acc_ref[...] + lax.dot_general(
        p.astype(BF16), vbuf[slot].astype(BF16), _NT, preferred_element_type=F32)
    m_ref[...] = m_new

    @pl.when(j == nblk - 1)
    def _():
        kn = kn_ref[0].astype(BF16).astype(F32)
        s_n = jnp.sum(qbd.astype(F32) * kn, axis=-1, keepdims=True) - (off + lfn_ref[0])
        m_p = m_ref[...]
        m_f = jnp.maximum(m_p, s_n)
        a = jnp.exp(m_p - m_f)
        p_n = jnp.exp(s_n - m_f)
        l_f = a * l_ref[...] + p_n
        acc = (a * acc_ref[...] + p_n * vn_ref[0].astype(BF16).astype(F32)) / l_f
        rr = lax.broadcasted_iota(jnp.int32, (nh, HEAD_DIM), 0) // (N_HEADS // N_KV_FOX)
        o = jnp.zeros((nh, HEAD_DIM), F32)
        for hh in range(N_KV_FOX):
            o = jnp.where(rr == hh, acc[:, hh * HEAD_DIM:(hh + 1) * HEAD_DIM], o)
        o_ref[0] = o.astype(BF16)


def _fox_decode_attn(page_table, q, k_new, v_new, lf_new, kc, vc, lc, page_base, pb):
    N, npages = page_table.shape
    pb = math.gcd(pb, npages)
    nblk = npages // pb
    page = kc.shape[-1]
    assert page == LANES and lc.shape[-1] == LANES
    khd = N_KV_FOX * HEAD_DIM
    grid_spec = pltpu.PrefetchScalarGridSpec(
        num_scalar_prefetch=1,
        grid=(N, nblk),
        in_specs=[pl.BlockSpec((1, N_HEADS, HEAD_DIM), lambda s, j, pt: (s, 0, 0)),
                  pl.BlockSpec((1, 1, khd), lambda s, j, pt: (s, 0, 0)),
                  pl.BlockSpec((1, 1, khd), lambda s, j, pt: (s, 0, 0)),
                  pl.BlockSpec((1, N_HEADS, 1), lambda s, j, pt: (s, 0, 0)),
                  pl.BlockSpec((page, page), lambda s, j, pt: (0, 0)),
                  pl.BlockSpec(memory_space=pl.ANY), pl.BlockSpec(memory_space=pl.ANY),
                  pl.BlockSpec(memory_space=pl.ANY)],
        out_specs=pl.BlockSpec((1, N_HEADS, HEAD_DIM), lambda s, j, pt: (s, 0, 0)),
        scratch_shapes=[pltpu.VMEM((2, khd, pb * page), F32), pltpu.VMEM((2, khd, pb * page), F32),
                        pltpu.VMEM((2, pb * N_HEADS, page), F32), pltpu.SemaphoreType.DMA((2, 3)),
                        pltpu.VMEM((N_HEADS, khd), BF16), pltpu.VMEM((N_HEADS, 1), F32),
                        pltpu.VMEM((N_HEADS, 1), F32), pltpu.VMEM((N_HEADS, khd), F32),
                        pltpu.VMEM((N_HEADS, LANES), F32)],
    )
    return pl.pallas_call(
        functools.partial(_fox_decode_kernel, pb=pb, nblk=nblk, page_base=page_base),
        grid_spec=grid_spec,
        out_shape=jax.ShapeDtypeStruct((N, N_HEADS, HEAD_DIM), BF16),
        compiler_params=_cparams("arbitrary", "arbitrary"),
        name="fox_decode_attn",
    )(page_table, q, k_new, v_new, lf_new, _upper_tri(page), kc, vc, lc)


def _rope_rows(x, cos, sin_signed):
    n = x.shape[1]
    half = HEAD_DIM // 2
    lane = lax.broadcasted_iota(jnp.int32, x.shape, 1)
    rot = jnp.where(lane % HEAD_DIM < half, pltpu.roll(x, n - half, 1), pltpu.roll(x, half, 1))
    reps = n // LANES
    return x * jnp.concatenate([cos] * reps, axis=1) + rot * jnp.concatenate([sin_signed] * reps, axis=1)


def _rope_cols(xT, cosT, sinT):
    half = HEAD_DIM // 2
    out = []
    for h in range(xT.shape[0] // HEAD_DIM):
        x1 = xT[h * HEAD_DIM:h * HEAD_DIM + half]
        x2 = xT[h * HEAD_DIM + half:(h + 1) * HEAD_DIM]
        out += [x1 * cosT - x2 * sinT, x2 * cosT + x1 * sinT]
    return jnp.concatenate(out, axis=0)


def _swa_proj_kernel(x_ref, g_ref, wq_ref, wkvT_ref, wkv_ref, cos_ref, sin_ref, cosT_ref, sinT_ref,
                     q_ref, kT_ref, vT_ref, kTb_ref, vb_ref, kn_ref, vn_ref, *, q_scale):
    h = _rms_bf16(x_ref[0], g_ref[...])
    q = jnp.dot(h, wq_ref[...], preferred_element_type=F32)
    q_ref[0] = (_rope_rows(q, cos_ref[...], sin_ref[...]) * q_scale).astype(BF16)
    kv = lax.dot_general(wkvT_ref[...], h, _NT, preferred_element_type=F32)
    nk = N_KV_SWA * HEAD_DIM
    k = _rope_cols(kv[:nk], cosT_ref[...], sinT_ref[...])
    kT_ref[0] = k
    vT_ref[0] = kv[nk:]
    kTb_ref[0] = k.astype(BF16)
    kvn = jnp.dot(h, wkv_ref[...], preferred_element_type=F32)
    kn_ref[0] = _rope_rows(kvn[:, :nk], cos_ref[...], sin_ref[...])
    vn_ref[0] = kvn[:, nk:]
    vb_ref[0] = kvn[:, nk:].astype(BF16)


def _swa_proj(x, g, wq, wkvT, wkv, pos, tm, q_scale):
    B, S, D = x.shape
    tm = min(tm, S)
    half = HEAD_DIM // 2
    inv = ROPE_THETA ** (-jnp.arange(half, dtype=F32) / half)
    ang = pos.astype(F32)[:, None] * inv[None, :]
    cos, sin = jnp.cos(ang), jnp.sin(ang)
    cos_rows = jnp.concatenate([cos] * (LANES // half), axis=1)
    sin_rows = jnp.concatenate([-sin, sin] * (LANES // HEAD_DIM), axis=1)
    nk = N_KV_SWA * HEAD_DIM
    const = lambda shape: pl.BlockSpec(shape, lambda b, t: (0,) * len(shape))
    tspec = pl.BlockSpec((1, nk, tm), lambda b, t: (b, 0, t))
    rspec = pl.BlockSpec((1, tm, nk), lambda b, t: (b, t, 0))
    assert nk == LANES
    return pl.pallas_call(
        functools.partial(_swa_proj_kernel, q_scale=q_scale),
        grid=(B, S // tm),
        in_specs=[pl.BlockSpec((1, tm, D), lambda b, t: (b, t, 0)), const(g.shape), const(wq.shape),
                  const(wkvT.shape), const(wkv.shape),
                  pl.BlockSpec((tm, LANES), lambda b, t: (t, 0)), pl.BlockSpec((tm, LANES), lambda b, t: (t, 0)),
                  pl.BlockSpec((half, tm), lambda b, t: (0, t)), pl.BlockSpec((half, tm), lambda b, t: (0, t))],
        out_specs=[pl.BlockSpec((1, tm, N_HEADS * HEAD_DIM), lambda b, t: (b, t, 0)),
                   tspec, tspec, tspec, rspec, rspec, rspec],
        out_shape=[jax.ShapeDtypeStruct((B, S, N_HEADS * HEAD_DIM), BF16),
                   jax.ShapeDtypeStruct((B, nk, S), F32), jax.ShapeDtypeStruct((B, nk, S), F32),
                   jax.ShapeDtypeStruct((B, nk, S), BF16), jax.ShapeDtypeStruct((B, S, nk), BF16),
                   jax.ShapeDtypeStruct((B, S, nk), F32), jax.ShapeDtypeStruct((B, S, nk), F32)],
        compiler_params=_cparams("arbitrary", "arbitrary"),
        name="swa_proj",
    )(x, g, wq, wkvT, wkv, cos_rows, sin_rows, cos.T, sin.T)


def _swa_prompt_attn_kernel(sink_ref, q_ref, kT_ref, vb_ref, o_ref, *, tq):
    qi = pl.program_id(1)
    G = N_HEADS // N_KV_SWA
    W = WINDOW
    r = lax.broadcasted_iota(jnp.int32, (G * W, 2 * W), 0) % W
    c = lax.broadcasted_iota(jnp.int32, (G * W, 2 * W), 1)
    ones = jnp.ones((2 * W, LANES), BF16)
    for n in range(tq // W):
        blk = qi * (tq // W) + n
        k0 = pl.multiple_of(jnp.maximum(blk - 1, 0) * W, W)
        delta = (blk * W + r) - (k0 + c)
        mask = jnp.where(delta >= 0, delta, W + 1) <= W
        outs = []
        for kv in range(N_KV_SWA):
            q8 = jnp.concatenate(
                [q_ref[0, n * W:(n + 1) * W, (kv * G + g) * HEAD_DIM:(kv * G + g + 1) * HEAD_DIM]
                 for g in range(G)], axis=0)
            sink = jnp.concatenate(
                [jnp.full((W, LANES), sink_ref[kv * G + g] * LOG2E, F32) for g in range(G)], axis=0)
            kT = kT_ref[0, kv * HEAD_DIM:(kv + 1) * HEAD_DIM, pl.ds(k0, 2 * W)]
            v = vb_ref[0, pl.ds(k0, 2 * W), kv * HEAD_DIM:(kv + 1) * HEAD_DIM]
            s = jnp.where(mask, jnp.dot(q8, kT, preferred_element_type=F32), NEG_INF)
            m = jnp.maximum(jnp.max(s, axis=-1, keepdims=True), sink)
            p = jnp.exp2(s - jnp.concatenate([m] * (2 * W // LANES), axis=1)).astype(BF16)
            den = jnp.dot(p, ones, preferred_element_type=F32) + jnp.exp2(sink - m)
            o8 = jnp.dot(p, v, preferred_element_type=F32) / den[:, :HEAD_DIM]
            outs += [o8[g * W:(g + 1) * W] for g in range(G)]
        o_ref[0, n * W:(n + 1) * W, :] = jnp.concatenate(outs, axis=1).astype(BF16)


def _swa_prompt_attn(sinks, q, kTb, vb, tq):
    B, S, _ = q.shape
    assert S % WINDOW == 0 and S >= 2 * WINDOW
    tq = min(tq, S)
    nk = N_KV_SWA * HEAD_DIM
    grid_spec = pltpu.PrefetchScalarGridSpec(
        num_scalar_prefetch=1,
        grid=(B, S // tq),
        in_specs=[pl.BlockSpec((1, tq, N_HEADS * HEAD_DIM), lambda b, i, sk: (b, i, 0)),
                  pl.BlockSpec((1, nk, S), lambda b, i, sk: (b, 0, 0)),
                  pl.BlockSpec((1, S, nk), lambda b, i, sk: (b, 0, 0))],
        out_specs=pl.BlockSpec((1, tq, N_HEADS * HEAD_DIM), lambda b, i, sk: (b, i, 0)),
    )
    return pl.pallas_call(
        functools.partial(_swa_prompt_attn_kernel, tq=tq),
        grid_spec=grid_spec,
        out_shape=jax.ShapeDtypeStruct((B, S, N_HEADS * HEAD_DIM), BF16),
        compiler_params=_cparams("arbitrary", "arbitrary"),
        name="swa_prompt_attn",
    )(sinks, q, kTb, vb)


def _swa_decode_kernel(q_ref, kc_ref, vc_ref, kn_ref, vn_ref, knT_ref, vnT_ref, sink_ref,
                       o_ref, ko_ref, vo_ref):
    G = N_HEADS // N_KV_SWA
    nk = N_KV_SWA * HEAD_DIM
    bs, _, wb = kc_ref.shape
    sink = sink_ref[...]
    rh = lax.broadcasted_iota(jnp.int32, (N_HEADS, HEAD_DIM), 0) // G
    lane = lax.broadcasted_iota(jnp.int32, (nk, wb), 1)
    seq_lane = lax.broadcasted_iota(jnp.int32, knT_ref.shape, 1)
    for i in range(bs):
        s_idx = pl.program_id(0) * bs + i
        q16 = q_ref[i].astype(F32)
        qt = jnp.concatenate([q16] * N_KV_SWA, axis=1)
        rr = lax.broadcasted_iota(jnp.int32, qt.shape, 0) // G
        cc = lax.broadcasted_iota(jnp.int32, qt.shape, 1) // HEAD_DIM
        qbd = jnp.where(rr == cc, qt, 0.0)
        kT = kc_ref[i]
        vT = vc_ref[i]
        s_p = jnp.dot(qbd.astype(BF16), kT.astype(BF16), preferred_element_type=F32)
        s_n = jnp.sum(qbd.astype(BF16).astype(F32) * kn_ref[i].astype(BF16).astype(F32), axis=-1, keepdims=True)
        m = jnp.maximum(jnp.maximum(jnp.max(s_p, axis=-1, keepdims=True), s_n), sink)
        p_p = jnp.exp(s_p - m)
        p_n = jnp.exp(s_n - m)
        den = jnp.sum(p_p, axis=-1, keepdims=True) + p_n + jnp.exp(sink - m)
        acc = lax.dot_general((p_p / den).astype(BF16), vT.astype(BF16), _NT, preferred_element_type=F32)
        acc = acc + (p_n / den).astype(BF16).astype(F32) * vn_ref[i].astype(BF16).astype(F32)
        o = jnp.zeros((N_HEADS, HEAD_DIM), F32)
        for kv in range(N_KV_SWA):
            o = jnp.where(rh == kv, acc[:, kv * HEAD_DIM:(kv + 1) * HEAD_DIM], o)
        o_ref[i] = o.astype(BF16)
        k_col = jnp.sum(jnp.where(seq_lane == s_idx, knT_ref[...], 0.0), axis=-1, keepdims=True)
        v_col = jnp.sum(jnp.where(seq_lane == s_idx, vnT_ref[...], 0.0), axis=-1, keepdims=True)
        ko_ref[i] = jnp.where(lane == wb - 1, k_col, pltpu.roll(kT, wb - 1, 1))
        vo_ref[i] = jnp.where(lane == wb - 1, v_col, pltpu.roll(vT, wb - 1, 1))


def _swa_decode_attn(q, kc, vc, k_new, v_new, k_newT, v_newT, sinks, bs):
    N, nk, wb = kc.shape
    assert wb == LANES and wb <= WINDOW
    bs = math.gcd(bs, N)
    seq3 = lambda shape: pl.BlockSpec((bs,) + shape, lambda s: (s, 0, 0))
    const = lambda shape: pl.BlockSpec(shape, lambda s: (0,) * len(shape))
    return pl.pallas_call(
        _swa_decode_kernel,
        grid=(N // bs,),
        in_specs=[seq3((N_HEADS, HEAD_DIM)), seq3((nk, wb)), seq3((nk, wb)), seq3((1, nk)), seq3((1, nk)),
                  const(k_newT.shape), const(v_newT.shape), const(sinks.shape)],
        out_specs=[seq3((N_HEADS, HEAD_DIM)), seq3((nk, wb)), seq3((nk, wb))],
        out_shape=[jax.ShapeDtypeStruct((N, N_HEADS, HEAD_DIM), BF16),
                   jax.ShapeDtypeStruct((N, nk, wb), F32), jax.ShapeDtypeStruct((N, nk, wb), F32)],
        compiler_params=_cparams("arbitrary"),
        name="swa_decode_attn",
    )(q, kc, vc, k_new, v_new, k_newT, v_newT, sinks)


def _silu(x):
    return x * (1.0 / (1.0 + jnp.exp(-x)))


def _ffn_chunks(F, n_chunks):
    if F % V7X_MXU_DIM:
        assert F % n_chunks == 0
        widths = [F // n_chunks] * n_chunks
    else:
        base, extra = divmod(F // V7X_MXU_DIM, n_chunks)
        widths = [(base + (i < extra)) * V7X_MXU_DIM for i in range(n_chunks)]
    bounds = [sum(widths[:i]) for i in range(n_chunks + 1)]
    return list(zip(bounds[:-1], bounds[1:]))


def _post_prompt_kernel(x_ref, o_ref, wo_ref, g_ref, win_ref, cw_ref, cb_ref, wout_ref, gf_ref,
                        y_ref, ulast_ref, halo_ref, ubuf_ref, *, n_chunks, final_norm):
    ti = pl.program_id(1)
    tm = x_ref.shape[1]
    F = wout_ref.shape[0]
    pad = ubuf_ref.shape[0] - tm

    @pl.when(ti == 0)
    def _():
        halo_ref[...] = jnp.zeros_like(halo_ref)

    xn = x_ref[0] + jnp.dot(o_ref[0], wo_ref[...], preferred_element_type=F32)
    h = _rms_bf16(xn, g_ref[...])
    y = jnp.zeros_like(xn)
    for f0, f1 in _ffn_chunks(F, n_chunks):
        fc = f1 - f0
        halves = []
        for c0 in (f0, F + f0):
            u = jnp.dot(h, win_ref[:, c0:c0 + fc], preferred_element_type=F32)
            ubuf_ref[0:pad, :fc] = halo_ref[:, c0:c0 + fc]
            ubuf_ref[pad:pad + tm, :fc] = u
            halo_ref[:, c0:c0 + fc] = u[tm - pad:]
            ulast_ref[0, :, c0:c0 + fc] = u[tm - pad:]
            conv = (cb_ref[:, c0:c0 + fc] + cw_ref[0:1, c0:c0 + fc] * ubuf_ref[pad - 2:pad - 2 + tm, :fc]
                    + cw_ref[1:2, c0:c0 + fc] * ubuf_ref[pad - 1:pad - 1 + tm, :fc]
                    + cw_ref[2:3, c0:c0 + fc] * u)
            halves.append(conv)
        a = (_silu(halves[0]) * halves[1]).astype(BF16)
        y = y + jnp.dot(a, wout_ref[f0:f1, :], preferred_element_type=F32)
    out = xn + y
    if final_norm:
        out = out * lax.rsqrt(jnp.mean(out * out, axis=-1, keepdims=True) + RMS_EPS) * gf_ref[...]
    y_ref[0] = out


def _post_prompt(x, o, wo, g, win, cw, cb, wout, gf, final_norm, tm, n_chunks):
    B, S, D = x.shape
    tm = min(tm, S)
    F2 = win.shape[1]
    pad = 8
    const = lambda shape: pl.BlockSpec(shape, lambda b, t: (0,) * len(shape))
    row = pl.BlockSpec((1, tm, D), lambda b, t: (b, t, 0))
    return pl.pallas_call(
        functools.partial(_post_prompt_kernel, n_chunks=n_chunks, final_norm=final_norm),
        grid=(B, S // tm),
        in_specs=[row, row, const(wo.shape), const(g.shape), const(win.shape), const(cw.shape),
                  const(cb.shape), const(wout.shape), const(gf.shape)],
        out_specs=[row, pl.BlockSpec((1, pad, F2), lambda b, t: (b, 0, 0))],
        out_shape=[jax.ShapeDtypeStruct((B, S, D), F32), jax.ShapeDtypeStruct((B, pad, F2), F32)],
        scratch_shapes=[pltpu.VMEM((pad, F2), F32),
                        pltpu.VMEM((tm + pad, max(f1 - f0 for f0, f1 in _ffn_chunks(F2 // 2, n_chunks))), F32)],
        compiler_params=_cparams("arbitrary", "arbitrary"),
        name="post_prompt",
    )(x, o, wo, g, win, cw, cb, wout, gf)


def _post_sample_kernel(x_ref, o_ref, wo_ref, g_ref, win_ref, cw_ref, cb_ref, wout_ref, gf_ref,
                        s0_ref, s1_ref, y_ref, u_ref, *, n_chunks, final_norm):
    F = wout_ref.shape[0]
    xn = x_ref[...] + jnp.dot(o_ref[...], wo_ref[...], preferred_element_type=F32)
    h = _rms_bf16(xn, g_ref[...])
    y = jnp.zeros_like(xn)
    for f0, f1 in _ffn_chunks(F, n_chunks):
        fc = f1 - f0
        halves = []
        for c0 in (f0, F + f0):
            u = jnp.dot(h, win_ref[:, c0:c0 + fc], preferred_element_type=F32)
            u_ref[:, c0:c0 + fc] = u
            halves.append(cb_ref[:, c0:c0 + fc] + cw_ref[0:1, c0:c0 + fc] * s0_ref[:, c0:c0 + fc]
                          + cw_ref[1:2, c0:c0 + fc] * s1_ref[:, c0:c0 + fc] + cw_ref[2:3, c0:c0 + fc] * u)
        a = (_silu(halves[0]) * halves[1]).astype(BF16)
        y = y + jnp.dot(a, wout_ref[f0:f1, :], preferred_element_type=F32)
    out = xn + y
    if final_norm:
        out = out * lax.rsqrt(jnp.mean(out * out, axis=-1, keepdims=True) + RMS_EPS) * gf_ref[...]
    y_ref[...] = out


def _post_sample(x, o, wo, g, win, cw, cb, wout, gf, s0, s1, final_norm, n_chunks):
    N, D = x.shape
    F2 = win.shape[1]
    return pl.pallas_call(
        functools.partial(_post_sample_kernel, n_chunks=n_chunks, final_norm=final_norm),
        out_shape=[jax.ShapeDtypeStruct((N, D), F32), jax.ShapeDtypeStruct((N, F2), F32)],
        compiler_params=_cparams(),
        name="post_sample",
    )(x, o, wo, g, win, cw, cb, wout, gf, s0, s1)


PROJ_TILE = 512
POST_TILE = 512
FFN_CHUNKS = 2
FOX_TQ = 512
FOX_KEY_SPAN = 4
SWA_TQ = 512
DECODE_PAGES_PER_BLOCK = 16
SWA_DECODE_SEQS = 8


def kernel(x_prompt, x_sample, cache_fox_k, cache_fox_v, cache_fox_logf, cache_swa_k, cache_swa_v, state_ffn_conv, page_table, norm_mixer, norm_ffn, norm_final, fox_w_qkv, fox_b_f, fox_w_o, swa_w_qkv, swa_sinks, swa_w_o, ffn_w_in, ffn_conv_w, ffn_conv_b, ffn_w_out):
    B, S, D = x_prompt.shape
    N = x_sample.shape[0]
    assert x_sample.shape[1] == 1
    depth = norm_mixer.shape[0]
    n_phys, page = cache_fox_k.shape[1], cache_fox_k.shape[2]
    past_len = page_table.shape[1] * page
    nq = N_HEADS * HEAD_DIM
    nkf = N_KV_FOX * HEAD_DIM
    nks = N_KV_SWA * HEAD_DIM
    F = ffn_w_out.shape[1]

    xp = x_prompt
    xs = x_sample.reshape(1, N, D)
    gfin = norm_final.reshape(1, D)

    kc = jnp.transpose(cache_fox_k, (0, 1, 3, 4, 2)).reshape(-1, nkf, page)
    vc = jnp.transpose(cache_fox_v, (0, 1, 3, 4, 2)).reshape(-1, nkf, page)
    lc = jnp.transpose(cache_fox_logf, (0, 1, 3, 2)).reshape(-1, N_HEADS, page)

    fkp, fvp, flp, fks, fvs, fls = [], [], [], [], [], []
    skp, svp, sks, svs = [], [], [], []
    cps, css = [], []
    for i in range(depth):
        j = i // 2
        g_mix = norm_mixer[i].reshape(1, D)
        if i % 2 == 0:
            w = fox_w_qkv[j]
            wq = w[:, :nq].astype(BF16)
            wkvT = w[:, nq:nq + 2 * nkf].T.astype(BF16)
            wv = w[:, nq + nkf:nq + 2 * nkf].astype(BF16)
            wfT = w[:, nq + 2 * nkf:].T.astype(BF16)
            bf = fox_b_f[j].reshape(N_HEADS, 1)
            q, kT, vT, kTa, va, lfT = _fox_proj(xp, g_mix, wq, wkvT, wv, wfT, bf, PROJ_TILE, SCALE * LOG2E)
            op = _fox_prompt_attn(q, kTa, va, FOX_TQ, FOX_KEY_SPAN)
            fkp.append(jnp.transpose(kT, (0, 3, 1, 2)))
            fvp.append(jnp.transpose(vT, (0, 3, 1, 2)))
            flp.append(jnp.transpose(lfT, (0, 2, 1)))

            qs, kTs, vTs, _, _, lfTs = _fox_proj(xs, g_mix, wq, wkvT, wv, wfT, bf, PROJ_TILE, SCALE)
            k_new = jnp.transpose(kTs[0], (2, 0, 1)).reshape(N, 1, nkf)
            v_new = jnp.transpose(vTs[0], (2, 0, 1)).reshape(N, 1, nkf)
            lf_new = jnp.transpose(lfTs[0], (1, 0)).reshape(N, N_HEADS, 1)
            os_ = _fox_decode_attn(page_table, qs.reshape(N, N_HEADS, HEAD_DIM), k_new, v_new, lf_new,
                                   kc, vc, lc, j * n_phys, DECODE_PAGES_PER_BLOCK)
            os_ = os_.reshape(N, nq)
            fks.append(jnp.transpose(kTs[0], (2, 0, 1)).reshape(N, 1, N_KV_FOX, HEAD_DIM))
            fvs.append(jnp.transpose(vTs[0], (2, 0, 1)).reshape(N, 1, N_KV_FOX, HEAD_DIM))
            fls.append(jnp.transpose(lfTs[0], (1, 0)).reshape(N, 1, N_HEADS))
            wo = fox_w_o[j].astype(BF16)
        else:
            w = swa_w_qkv[j]
            wq = w[:, :nq].astype(BF16)
            wkv = w[:, nq:].astype(BF16)
            wkvT = w[:, nq:].T.astype(BF16)
            sinks = swa_sinks[j]
            q, kT, vT, kTb, vb, _, _ = _swa_proj(xp, g_mix, wq, wkvT, wkv, jnp.arange(S), PROJ_TILE,
                                                 SCALE * LOG2E)
            op = _swa_prompt_attn(sinks, q, kTb, vb, SWA_TQ)
            keep = min(WINDOW, S)
            skp.append(jnp.transpose(kT[:, :, S - keep:].reshape(B, N_KV_SWA, HEAD_DIM, keep), (0, 3, 1, 2)))
            svp.append(jnp.transpose(vT[:, :, S - keep:].reshape(B, N_KV_SWA, HEAD_DIM, keep), (0, 3, 1, 2)))

            pos_s = jnp.full((N,), past_len, jnp.int32)
            qs, kTs, vTs, _, _, kns, vns = _swa_proj(xs, g_mix, wq, wkvT, wkv, pos_s, PROJ_TILE, SCALE)
            wb = cache_swa_k.shape[2]
            kcs = jnp.transpose(cache_swa_k[j], (0, 2, 3, 1)).reshape(N, nks, wb)
            vcs = jnp.transpose(cache_swa_v[j], (0, 2, 3, 1)).reshape(N, nks, wb)
            os_, ko, vo = _swa_decode_attn(qs.reshape(N, N_HEADS, HEAD_DIM), kcs, vcs,
                                           kns.reshape(N, 1, nks), vns.reshape(N, 1, nks),
                                           kTs[0], vTs[0], sinks.reshape(N_HEADS, 1), SWA_DECODE_SEQS)
            os_ = os_.reshape(N, nq)
            sks.append(jnp.transpose(ko.reshape(N, N_KV_SWA, HEAD_DIM, wb), (0, 3, 1, 2)))
            svs.append(jnp.transpose(vo.reshape(N, N_KV_SWA, HEAD_DIM, wb), (0, 3, 1, 2)))
            wo = swa_w_o[j].astype(BF16)

        g_ffn = norm_ffn[i].reshape(1, D)
        win = ffn_w_in[i].astype(BF16)
        wout = ffn_w_out[i].astype(BF16)
        cw = ffn_conv_w[i]
        cb = ffn_conv_b[i].reshape(1, 2 * F)
        last = i == depth - 1
        xp, ulast = _post_prompt(xp, op, wo, g_ffn, win, cw, cb, wout, gfin, last, POST_TILE, FFN_CHUNKS)
        cps.append(ulast[:, -(CONV_W - 1):])
        st = state_ffn_conv[i]
        xs2, us = _post_sample(xs[0], os_, wo, g_ffn, win, cw, cb, wout, gfin, st[:, 0], st[:, 1],
                               last, FFN_CHUNKS)
        xs = xs2.reshape(1, N, D)
        css.append(jnp.stack([st[:, 1], us], axis=1))

    return (xp, xs.reshape(N, 1, D),
            jnp.stack(fkp), jnp.stack(fvp), jnp.stack(flp),
            jnp.stack(fks), jnp.stack(fvs), jnp.stack(fls),
            jnp.stack(skp), jnp.stack(svp), jnp.stack(sks), jnp.stack(svs),
            jnp.stack(cps), jnp.stack(css))
```
